```python
import math
import jax
import jax.numpy as jnp
from jax import lax
import numpy as np

D_MODEL = 2048
BATCH = 16
SEQ = 2048
DEPTH = 4

GRID_W = 64
CTX_LEN = 256

NA_HEADS = 8
NA_HEAD_DIM = 128
NA_WIN_ROWS = 8
NA_WIN_COLS = 16
ROPE_THETA = 10000.0

FN_GROUPS = 4
FN_GROUP_DIM = 256

SSM_HEADS = 32
SSM_HEAD_DIM = 64
SSM_INNER = SSM_HEADS * SSM_HEAD_DIM
SSM_GROUPS = 4
SSM_STATE = 128
SSM_CONV = 5
SSM_CHUNK = 128

FFN_HIDDEN = -(-8 * D_MODEL // (3 * 256)) * 256

NA_WIDTH = NA_HEADS * NA_HEAD_DIM
FN_WIDTH = FN_GROUPS * FN_GROUP_DIM
CONV_DIM = SSM_INNER + 2 * SSM_GROUPS * SSM_STATE
PROJ_WIDTH = 3 * NA_WIDTH + FN_WIDTH + CONV_DIM + SSM_INNER + 2 * SSM_HEADS + 3 * D_MODEL

kernel_name = 'hybrid_natten_fnet_ssd_dit_trunk'


def _layer_norm(x, g=None, b=None, eps=1e-6):
    xf = x.astype(jnp.float32)
    mu = jnp.mean(xf, axis=-1, keepdims=True)
    var = jnp.mean(jnp.square(xf - mu), axis=-1, keepdims=True)
    y = (xf - mu) * lax.rsqrt(var + eps)
    if g is not None:
        y = y * g.astype(jnp.float32) + b.astype(jnp.float32)
    return y.astype(x.dtype)


def _modulate(x, shift, scale):
    return _layer_norm(x) * (1 + scale) + shift


def _split_proj(p):
    sizes = (NA_WIDTH, NA_WIDTH, NA_WIDTH, FN_WIDTH, CONV_DIM, SSM_INNER, 2 * SSM_HEADS)
    cuts = np.cumsum(sizes).tolist()
    return jnp.split(p, cuts, axis=-1)


def _axial_rope(n_tokens):
    t = jnp.arange(n_tokens)
    row = (t // GRID_W).astype(jnp.float32)
    col = (t % GRID_W).astype(jnp.float32)
    n_freq = NA_HEAD_DIM // 4
    inv = ROPE_THETA ** (-jnp.arange(n_freq, dtype=jnp.float32) / n_freq)
    ang = jnp.stack([row[:, None] * inv, col[:, None] * inv], axis=1)
    return jnp.cos(ang), jnp.sin(ang)


def _apply_rope(t, cos, sin):
    b, s, h, d = t.shape
    tf = t.astype(jnp.float32).reshape(b, s, h, 2, 2, d // 4)
    cs = cos[None, :, None]
    sn = sin[None, :, None]
    t1, t2 = tf[..., 0, :], tf[..., 1, :]
    out = jnp.stack([t1 * cs - t2 * sn, t2 * cs + t1 * sn], axis=-2)
    return out.reshape(b, s, h, d).astype(t.dtype)


def _natten_latent(q_rot, q_raw, k_rot, v, k_ctx, v_ctx, rpb):
    b, s, h, d = q_rot.shape
    rows = s // GRID_W
    kr = min(NA_WIN_ROWS, rows)
    kc = NA_WIN_COLS
    scale = d ** -0.5
    grid = lambda t: t.reshape(b, rows, GRID_W, h, d)
    k_g, v_g = grid(k_rot), grid(v)
    qr_rows = jnp.moveaxis(grid(q_rot), 1, 0)
    qp_rows = jnp.moveaxis(grid(q_raw), 1, 0)
    r_idx = jnp.arange(rows)
    r_start = jnp.clip(r_idx - kr // 2, 0, rows - kr)
    col = jnp.arange(GRID_W)
    c_start = jnp.clip(col - kc // 2, 0, GRID_W - kc)
    col_in = (col[None, :] >= c_start[:, None]) & (col[None, :] < c_start[:, None] + kc)
    dc = jnp.clip(col[None, :] - col[:, None], -(kc - 1), kc - 1) + (NA_WIN_COLS - 1)

    def row_block(xs):
        qr, qp, r, rs = xs
        k_blk = lax.dynamic_slice_in_dim(k_g, rs, kr, axis=1)
        v_blk = lax.dynamic_slice_in_dim(v_g, rs, kr, axis=1)
        s_lat = jnp.einsum('bqhd,bikhd->bhqik', qr, k_blk).astype(jnp.float32) * scale
        dr = rs + jnp.arange(kr) - r + (NA_WIN_ROWS - 1)
        bias = rpb[:, dr[None, :, None], dc[:, None, :]].astype(jnp.float32)
        s_lat = jnp.where(col_in[:, None, :], s_lat + bias, -jnp.inf)
        s_ctx = jnp.einsum('bqhd,bchd->bhqc', qp, k_ctx).astype(jnp.float32) * scale
        logits = jnp.concatenate([s_lat.reshape(b, h, GRID_W, kr * GRID_W), s_ctx], axis=-1)
        p = jax.nn.softmax(logits, axis=-1)
        p_lat = p[..., :kr * GRID_W].reshape(b, h, GRID_W, kr, GRID_W).astype(v.dtype)
        p_ctx = p[..., kr * GRID_W:].astype(v.dtype)
        return (jnp.einsum('bhqik,bikhd->bqhd', p_lat, v_blk)
                + jnp.einsum('bhqc,bchd->bqhd', p_ctx, v_ctx))

    out = lax.map(row_block, (qr_rows, qp_rows, r_idx, r_start))
    return jnp.moveaxis(out, 0, 1).reshape(b, s, h * d)


def _ctx_attention(q, k, v):
    b, n, h, d = q.shape
    s = jnp.einsum('bqhd,bkhd->bhqk', q, k).astype(jnp.float32) * (d ** -0.5)
    p = jax.nn.softmax(s, axis=-1).astype(v.dtype)
    return jnp.einsum('bhqk,bkhd->bqhd', p, v).reshape(b, n, h * d)


def _fourier_mix(u):
    b, s, _ = u.shape
    ug = u.astype(jnp.float32).reshape(b, s, FN_GROUPS, FN_GROUP_DIM)
    f = jnp.fft.fft2(ug, axes=(1, 3), norm='ortho').real
    return f.reshape(b, s, FN_WIDTH).astype(u.dtype)


def _dw_conv(x, w, bias):
    out = lax.conv_general_dilated(
        x, w[:, None, :].astype(x.dtype), window_strides=(1,),
        padding=[(SSM_CONV // 2, SSM_CONV // 2)],
        dimension_numbers=('NWC', 'WIO', 'NWC'), feature_group_count=x.shape[-1])
    return out + bias.astype(x.dtype)


def _ssd_prep(xbc, dt_raw, conv_w, conv_b, dt_bias):
    xbc = jax.nn.silu(_dw_conv(xbc, conv_w, conv_b))
    b, l, _ = xbc.shape
    xs, bm, cm = jnp.split(xbc, [SSM_INNER, SSM_INNER + SSM_GROUPS * SSM_STATE], axis=-1)
    xs = xs.reshape(b, l, SSM_HEADS, SSM_HEAD_DIM)
    bm = bm.reshape(b, l, SSM_GROUPS, SSM_STATE)
    cm = cm.reshape(b, l, SSM_GROUPS, SSM_STATE)
    dt = jax.nn.softplus(dt_raw.astype(jnp.float32) + dt_bias.reshape(-1).astype(jnp.float32))
    return xs, bm, cm, dt.reshape(b, l, 2, SSM_HEADS)


def _segsum(a):
    t = a.shape[-1]
    ax = jnp.broadcast_to(a[..., :, None], a.shape + (t,))
    ax = jnp.where(jnp.tril(jnp.ones((t, t), bool), -1), ax, 0)
    seg = jnp.cumsum(ax, axis=-2)
    return jnp.where(jnp.tril(jnp.ones((t, t), bool), 0), seg, -jnp.inf)


def _ssd_chunked(x, dt, A, bm, cm, h0):
    b, l, h, p = x.shape
    g, n = bm.shape[2], bm.shape[3]
    r = h // g
    q = SSM_CHUNK
    nc = l // q
    xd = (x * dt[..., None]).reshape(b, nc, q, g, r, p)
    a = jnp.transpose((dt * A).reshape(b, nc, q, g, r), (0, 3, 4, 1, 2))
    bc = bm.reshape(b, nc, q, g, n)
    cc = cm.reshape(b, nc, q, g, n)
    a_cum = jnp.cumsum(a, axis=-1)
    lmat = jnp.exp(_segsum(a))
    cb = jnp.einsum('bclgn,bcsgn->bgcls', cc, bc)
    y_diag = jnp.einsum('bgcls,bgrcls,bcsgrp->bclgrp', cb, lmat, xd)
    decay_states = jnp.exp(a_cum[..., -1:] - a_cum)
    states = jnp.einsum('bcsgn,bgrcs,bcsgrp->bcgrpn', bc, decay_states, xd)
    states = jnp.concatenate([h0.reshape(b, 1, g, r, p, n).astype(states.dtype), states], axis=1)
    chunk_a = jnp.pad(a_cum[..., -1], ((0, 0), (0, 0), (0, 0), (1, 0)))
    decay_chunk = jnp.exp(_segsum(chunk_a))
    new_states = jnp.einsum('bgrzc,bcgrpn->bzgrpn', decay_chunk, states)
    y_off = jnp.einsum('bclgn,bcgrpn,bgrcl->bclgrp', cc, new_states[:, :-1], jnp.exp(a_cum))
    y = (y_diag + y_off).reshape(b, l, h, p)
    return y, new_states[:, -1].reshape(b, h, p, n)


def _ssd_final_state(x, dt, A, bm):
    b, l, h, p = x.shape
    a_cum = jnp.cumsum(dt * A, axis=1)
    w = dt * jnp.exp(a_cum[:, -1:] - a_cum)
    xd = (x * w[..., None]).reshape(b, l, SSM_GROUPS, h // SSM_GROUPS, p)
    return jnp.einsum('blgn,blgrp->bgrpn', bm, xd).reshape(b, h, p, SSM_STATE)


def _bi_ssd(xs, bm, cm, dt, A, h0_f, h0_b):
    rev = lambda t: t[:, ::-1]
    y_f, h_f = _ssd_chunked(xs, dt[:, :, 0], A[0], bm, cm, h0_f)
    y_b, h_b = _ssd_chunked(rev(xs), rev(dt[:, :, 1]), A[1], rev(bm), rev(cm), h0_b)
    return y_f + rev(y_b), h_f, h_b


def _ssd_out(y, xs, z, d_skip, norm_w):
    b, l, h, p = xs.shape
    y = (y + d_skip[:, None] * xs).reshape(b, l, SSM_INNER)
    gy = (y * jax.nn.silu(z.astype(jnp.float32))).astype(jnp.float32).reshape(b, l, SSM_GROUPS, -1)
    gy = gy * lax.rsqrt(jnp.mean(jnp.square(gy), axis=-1, keepdims=True) + 1e-5)
    return (gy.reshape(b, l, SSM_INNER) * norm_w.astype(jnp.float32)).astype(xs.dtype)


def _merge(o_na, o_fn, o_ssm, gate_logits, w_br_na, w_br_fn, w_br_ssm, w_out):
    g_na, g_fn, g_ssm = jnp.split(jax.nn.sigmoid(gate_logits), 3, axis=-1)
    m = g_na * (o_na @ w_br_na) + g_fn * (o_fn @ w_br_fn) + g_ssm * (o_ssm @ w_br_ssm)
    return m @ w_out


def _swiglu(h, w_gate, w_up, w_down):
    return (jax.nn.silu(h @ w_gate) * (h @ w_up)) @ w_down


def _mixers(hx, hc, rope, w_in, rpb, conv_w, conv_b, a_log, dt_bias, d_skip, norm_w,
            w_br_na, w_br_fn, w_br_ssm, w_out, with_ctx_out):
    b = hx.shape[0]
    cos, sin = rope
    heads = lambda t: t.reshape(t.shape[0], t.shape[1], NA_HEADS, NA_HEAD_DIM)
    qx, kx, vx, ux, xbcx, zx, dtx, gx = _split_proj(hx @ w_in)
    qc, kc, vc, uc, xbcc, zc, dtc, gc = _split_proj(hc @ w_in)
    k_ctx, v_ctx = heads(kc), heads(vc)
    qx_h = heads(qx)
    o_na = _natten_latent(_apply_rope(qx_h, cos, sin), qx_h, _apply_rope(heads(kx), cos, sin),
                          heads(vx), k_ctx, v_ctx, rpb)
    o_fn = _fourier_mix(ux)
    A = -jnp.exp(a_log.astype(jnp.float32))
    xs_c, b_c, c_c, dt_c = _ssd_prep(xbcc, dtc, conv_w, conv_b, dt_bias)
    if with_ctx_out:
        zeros = jnp.zeros((b, SSM_HEADS, SSM_HEAD_DIM, SSM_STATE), jnp.float32)
        y_c, h_f, h_b = _bi_ssd(xs_c, b_c, c_c, dt_c, A, zeros, zeros)
    else:
        h_f = _ssd_final_state(xs_c, dt_c[:, :, 0], A[0], b_c)
        h_b = _ssd_final_state(xs_c[:, ::-1], dt_c[:, ::-1, 1], A[1], b_c[:, ::-1])
    xs_x, b_x, c_x, dt_x = _ssd_prep(xbcx, dtx, conv_w, conv_b, dt_bias)
    y_x, _, _ = _bi_ssd(xs_x, b_x, c_x, dt_x, A, h_f, h_b)
    o_ssm = _ssd_out(y_x, xs_x, zx, d_skip, norm_w)
    mix_x = _merge(o_na, o_fn, o_ssm, gx, w_br_na, w_br_fn, w_br_ssm, w_out)
    if not with_ctx_out:
        return mix_x, None
    o_na_c = _ctx_attention(heads(qc), k_ctx, v_ctx)
    o_fn_c = _fourier_mix(uc)
    o_ssm_c = _ssd_out(y_c, xs_c, zc, d_skip, norm_w)
    mix_c = _merge(o_na_c, o_fn_c, o_ssm_c, gc, w_br_na, w_br_fn, w_br_ssm, w_out)
    return mix_x, mix_c


def setup_inputs(seed: int = 0) -> dict:
    key = jax.random.key(seed)
    ks = jax.random.split(key, 24)
    f32 = jnp.float32
    nrm = lambda k, shape, s: s * jax.random.normal(k, shape, f32)
    beta = (8.0 * DEPTH) ** -0.25
    dt0 = jnp.exp(jax.random.uniform(ks[10], (DEPTH, 2, SSM_HEADS), f32, math.log(1e-3), math.log(1e-1)))
    return {
        'x': nrm(ks[0], (BATCH, SEQ, D_MODEL), 1.0),
        'c': nrm(ks[1], (BATCH, D_MODEL), 1.0),
        'ctx': nrm(ks[2], (BATCH, CTX_LEN, D_MODEL), 1.0),
        'c_ctx': nrm(ks[3], (D_MODEL,), 1.0),
        'w_mod': nrm(ks[4], (DEPTH, D_MODEL, 6 * D_MODEL), 0.5 * D_MODEL ** -0.5),
        'w_in': nrm(ks[5], (DEPTH, D_MODEL, PROJ_WIDTH), D_MODEL ** -0.5),
        'na_rpb': nrm(ks[6], (DEPTH, NA_HEADS, 2 * NA_WIN_ROWS - 1, 2 * NA_WIN_COLS - 1), 0.02),
        'ssm_conv_w': nrm(ks[7], (DEPTH, SSM_CONV, CONV_DIM), SSM_CONV ** -0.5),
        'ssm_conv_b': nrm(ks[8], (DEPTH, CONV_DIM), 0.01),
        'ssm_a_log': jnp.log(jax.random.uniform(ks[9], (DEPTH, 2, SSM_HEADS), f32, 1.0, 16.0)),
        'ssm_dt_bias': dt0 + jnp.log(-jnp.expm1(-dt0)),
        'ssm_d': 1.0 + nrm(ks[11], (DEPTH, SSM_HEADS), 0.1),
        'ssm_norm_w': 1.0 + nrm(ks[12], (DEPTH, SSM_INNER), 0.1),
        'w_br_na': nrm(ks[13], (DEPTH, NA_WIDTH, D_MODEL), NA_WIDTH ** -0.5),
        'w_br_fn': nrm(ks[14], (DEPTH, FN_WIDTH, D_MODEL), FN_WIDTH ** -0.5),
        'w_br_ssm': nrm(ks[15], (DEPTH, SSM_INNER, D_MODEL), SSM_INNER ** -0.5),
        'w_out': nrm(ks[16], (DEPTH, D_MODEL, D_MODEL), beta * D_MODEL ** -0.5),
        'ln1_g': 1.0 + nrm(ks[17], (DEPTH, D_MODEL), 0.1),
        'ln1_b': nrm(ks[18], (DEPTH, D_MODEL), 0.01),
        'w_ffn_gate': nrm(ks[19], (DEPTH, D_MODEL, FFN_HIDDEN), D_MODEL ** -0.5),
        'w_ffn_up': nrm(ks[20], (DEPTH, D_MODEL, FFN_HIDDEN), D_MODEL ** -0.5),
        'w_ffn_down': nrm(ks[21], (DEPTH, FFN_HIDDEN, D_MODEL), beta * FFN_HIDDEN ** -0.5),
        'ln2_g': 1.0 + nrm(ks[22], (DEPTH, D_MODEL), 0.1),
        'ln2_b': nrm(ks[23], (DEPTH, D_MODEL), 0.01),
    }


def reference(x, c, ctx, c_ctx, w_mod, w_in, na_rpb, ssm_conv_w, ssm_conv_b, ssm_a_log,
              ssm_dt_bias, ssm_d, ssm_norm_w, w_br_na, w_br_fn, w_br_ssm, w_out, ln1_g, ln1_b,
              w_ffn_gate, w_ffn_up, w_ffn_down, ln2_g, ln2_b):
    alpha = (2.0 * DEPTH) ** 0.25
    rope = _axial_rope(x.shape[1])
    c_act = jax.nn.silu(c)
    cc_act = jax.nn.silu(c_ctx)
    for l in range(DEPTH):
        with_ctx = l < DEPTH - 1
        sh1, sc1, g1, sh2, sc2, g2 = [m[:, None, :] for m in jnp.split(c_act @ w_mod[l], 6, axis=-1)]
        csh1, csc1, cg1, csh2, csc2, cg2 = jnp.split(cc_act @ w_mod[l], 6, axis=-1)
        mix_x, mix_c = _mixers(_modulate(x, sh1, sc1), _modulate(ctx, csh1, csc1), rope,
                               w_in[l], na_rpb[l], ssm_conv_w[l], ssm_conv_b[l], ssm_a_log[l],
                               ssm_dt_bias[l], ssm_d[l], ssm_norm_w[l], w_br_na[l], w_br_fn[l],
                               w_br_ssm[l], w_out[l], with_ctx)
        x = _layer_norm(alpha * x + g1 * mix_x, ln1_g[l], ln1_b[l])
        x = _layer_norm(alpha * x + g2 * _swiglu(_modulate(x, sh2, sc2), w_ffn_gate[l], w_ffn_up[l],
                                                 w_ffn_down[l]), ln2_g[l], ln2_b[l])
        if with_ctx:
            ctx = _layer_norm(alpha * ctx + cg1 * mix_c, ln1_g[l], ln1_b[l])
            ctx = _layer_norm(alpha * ctx + cg2 * _swiglu(_modulate(ctx, csh2, csc2), w_ffn_gate[l],
                                                          w_ffn_up[l], w_ffn_down[l]), ln2_g[l], ln2_b[l])
    return x
```

```python
import functools
import math

import jax
import jax.numpy as jnp
import numpy as np
from jax import lax
from jax.experimental import pallas as pl
from jax.experimental.pallas import tpu as pltpu

F32 = jnp.float32
BF16 = jnp.bfloat16

GRID_W = 64
NA_WIN_ROWS = 8
NA_WIN_COLS = 16
NA_HEAD_DIM = 128
ROPE_THETA = 10000.0
FN_GROUPS = 4
SSM_GROUPS = 4
SSM_HEAD_DIM = 64
SSM_CHUNK = 128
LN_EPS = 1e-6
RMS_EPS = 1e-5
MASK_VALUE = -1e30

V7X_VMEM_BYTES = 64 * 1024 * 1024
VMEM_LIMIT_BYTES = V7X_VMEM_BYTES - 8 * 1024 * 1024
MOD_ROWS_ALIGN = 16


def _params(*sem):
    return pltpu.CompilerParams(dimension_semantics=sem, vmem_limit_bytes=VMEM_LIMIT_BYTES)


def _tile(n, *candidates):
    for c in candidates:
        if n % c == 0:
            return c
    return n


def _sigmoid(x):
    return 1.0 / (1.0 + jnp.exp(-x))


def _silu(x):
    return x * _sigmoid(x)


def _softplus(x):
    return jnp.maximum(x, 0.0) + jnp.log(1.0 + jnp.exp(-jnp.abs(x)))


def _ln(x):
    mu = jnp.mean(x, axis=-1, keepdims=True)
    xc = x - mu
    var = jnp.mean(xc * xc, axis=-1, keepdims=True)
    return xc * lax.rsqrt(var + LN_EPS)


def _dot(a, b):
    return jnp.dot(a, b, preferred_element_type=F32)


def _dot_nt(a, b):
    return lax.dot_general(a, b, (((1,), (1,)), ((), ())), preferred_element_type=F32)


def _dot_tn(a, b):
    return lax.dot_general(a, b, (((0,), (0,)), ((), ())), preferred_element_type=F32)


def _split3(a):
    hi = a.astype(BF16)
    r = a - hi.astype(F32)
    mid = r.astype(BF16)
    lo = (r - mid.astype(F32)).astype(BF16)
    return hi, mid, lo


def _dot_sel_r(sel, a):
    hi, mid, lo = _split3(a)
    return _dot(sel, hi) + _dot(sel, mid) + _dot(sel, lo)


def _dot_sel_l(a, sel):
    hi, mid, lo = _split3(a)
    return _dot(hi, sel) + _dot(mid, sel) + _dot(lo, sel)


def _mods_kernel(c_ref, w_ref, o_ref):
    a = _silu(c_ref[...]).astype(BF16)
    o_ref[0] = _dot(a, w_ref[0].astype(BF16))


def _mods(c_rows, w_mod):
    depth, d, n = w_mod.shape
    r = c_rows.shape[0]
    tn = _tile(n, 1024, 512, 256, 128)
    return pl.pallas_call(
        _mods_kernel,
        grid=(depth, n // tn),
        in_specs=[pl.BlockSpec((r, d), lambda l, j: (0, 0)),
                  pl.BlockSpec((1, d, tn), lambda l, j: (l, 0, j))],
        out_specs=pl.BlockSpec((1, r, tn), lambda l, j: (l, 0, j)),
        out_shape=jax.ShapeDtypeStruct((depth, r, n), F32),
        compiler_params=_params("parallel", "parallel"),
    )(c_rows, w_mod)


def _inproj_kernel(x_ref, sh_ref, sc_ref, w_ref, wdt_ref, wdtt_ref, o_ref, dt_ref, dtt_ref, h_scr):
    @pl.when(pl.program_id(1) == 0)
    def _():
        h = (_ln(x_ref[...]) * (1.0 + sc_ref[0]) + sh_ref[0]).astype(BF16)
        h_scr[...] = h
        dt_ref[...] = _dot(h, wdt_ref[...])
        dtt_ref[...] = _dot_nt(wdtt_ref[...], h)

    o_ref[...] = _dot(h_scr[...], w_ref[...]).astype(BF16)


def _inproj(x2d, shift, scale, rows_per_mod, w_main, w_dt, w_dtt):
    m, d = x2d.shape
    n = w_main.shape[1]
    ndt = w_dt.shape[1]
    tm = _tile(m, 1024, 512, 256, 128)
    tn = _tile(n, 1024, 512, 256, 128)
    assert rows_per_mod % tm == 0
    mod_spec = pl.BlockSpec((1, 1, d), lambda i, j: (i * tm // rows_per_mod, 0, 0))
    return pl.pallas_call(
        _inproj_kernel,
        grid=(m // tm, n // tn),
        in_specs=[pl.BlockSpec((tm, d), lambda i, j: (i, 0)),
                  mod_spec, mod_spec,
                  pl.BlockSpec((d, tn), lambda i, j: (0, j)),
                  pl.BlockSpec((d, ndt), lambda i, j: (0, 0)),
                  pl.BlockSpec((ndt, d), lambda i, j: (0, 0))],
        out_specs=[pl.BlockSpec((tm, tn), lambda i, j: (i, j)),
                   pl.BlockSpec((tm, ndt), lambda i, j: (i, 0)),
                   pl.BlockSpec((ndt, tm), lambda i, j: (0, i))],
        out_shape=[jax.ShapeDtypeStruct((m, n), BF16),
                   jax.ShapeDtypeStruct((m, ndt), F32),
                   jax.ShapeDtypeStruct((ndt, m), F32)],
        scratch_shapes=[pltpu.VMEM((tm, d), BF16)],
        compiler_params=_params("parallel", "arbitrary"),
    )(x2d, shift, scale, w_main, w_dt, w_dtt)


def _na_kernel(q_ref, k_ref, v_ref, kc_ref, vc_ref, cos_ref, sin_ref, bias_ref, o_ref,
               qr_scr, kr_scr, *, rows, scale):
    s, hd = qr_scr.shape
    half = hd // 4
    lane = lax.broadcasted_iota(jnp.int32, (s, hd), 1)
    first = (lane % (2 * half)) < half

    def rope(t):
        partner = jnp.where(first, pltpu.roll(t, hd - half, 1), pltpu.roll(t, half, 1))
        return t * cos_ref[...] + partner * sin_ref[...]

    qr_scr[...] = rope(q_ref[0].astype(F32)).astype(BF16)
    kr_scr[...] = rope(k_ref[0].astype(F32)).astype(BF16)
    kc = kc_ref[0]
    vc = vc_ref[0]
    nwin = NA_WIN_ROWS * GRID_W

    def body(r, carry):
        rs = jnp.clip(r - NA_WIN_ROWS // 2, 0, rows - NA_WIN_ROWS)
        q0 = pl.multiple_of(r * GRID_W, GRID_W)
        k0 = pl.multiple_of(rs * GRID_W, GRID_W)
        qr = qr_scr[pl.ds(q0, GRID_W), :]
        qp = q_ref[0, pl.ds(q0, GRID_W), :]
        kw = kr_scr[pl.ds(k0, nwin), :]
        vw = v_ref[0, pl.ds(k0, nwin), :]
        s_lat = _dot_nt(qr, kw) * scale + bias_ref[0, rs - r + NA_WIN_ROWS - 1]
        s_ctx = _dot_nt(qp, kc) * scale
        mx = jnp.maximum(jnp.max(s_lat, axis=-1, keepdims=True), jnp.max(s_ctx, axis=-1, keepdims=True))
        p_lat = jnp.exp(s_lat - mx)
        p_ctx = jnp.exp(s_ctx - mx)
        den = jnp.sum(p_lat, axis=-1, keepdims=True) + jnp.sum(p_ctx, axis=-1, keepdims=True)
        o = _dot(p_lat.astype(BF16), vw) + _dot(p_ctx.astype(BF16), vc)
        o_ref[0, pl.ds(q0, GRID_W), :] = (o / den).astype(BF16)
        return carry

    lax.fori_loop(0, rows, body, 0)


def _na_latent(px, pc, cos_t, sin_t, bias_t, *, heads, off_q, off_k, off_v):
    b, s, _ = px.shape
    n_ctx = pc.shape[1]
    hd = NA_HEAD_DIM
    rows = s // GRID_W
    assert rows >= NA_WIN_ROWS and s % GRID_W == 0
    nwin = NA_WIN_ROWS * GRID_W
    col = lambda off: (lambda bi, h: (bi, 0, off // hd + h))
    kern = functools.partial(_na_kernel, rows=rows, scale=hd ** -0.5)
    return pl.pallas_call(
        kern,
        grid=(b, heads),
        in_specs=[pl.BlockSpec((1, s, hd), col(off_q)),
                  pl.BlockSpec((1, s, hd), col(off_k)),
                  pl.BlockSpec((1, s, hd), col(off_v)),
                  pl.BlockSpec((1, n_ctx, hd), col(off_k)),
                  pl.BlockSpec((1, n_ctx, hd), col(off_v)),
                  pl.BlockSpec((s, hd), lambda bi, h: (0, 0)),
                  pl.BlockSpec((s, hd), lambda bi, h: (0, 0)),
                  pl.BlockSpec((1, NA_WIN_ROWS, GRID_W, nwin), lambda bi, h: (h, 0, 0, 0))],
        out_specs=pl.BlockSpec((1, s, hd), lambda bi, h: (bi, 0, h)),
        out_shape=jax.ShapeDtypeStruct((b, s, heads * hd), BF16),
        scratch_shapes=[pltpu.VMEM((s, hd), BF16), pltpu.VMEM((s, hd), BF16)],
        compiler_params=_params("parallel", "parallel"),
    )(px, px, px, pc, pc, cos_t, sin_t, bias_t)


def _ctx_attn_kernel(q_ref, k_ref, v_ref, o_ref, *, scale):
    s = _dot_nt(q_ref[0], k_ref[0]) * scale
    p = jnp.exp(s - jnp.max(s, axis=-1, keepdims=True))
    den = jnp.sum(p, axis=-1, keepdims=True)
    o_ref[0] = (_dot(p.astype(BF16), v_ref[0]) / den).astype(BF16)


def _ctx_attn(pc, *, heads, off_q, off_k, off_v):
    b, n, _ = pc.shape
    hd = NA_HEAD_DIM
    col = lambda off: (lambda bi, h: (bi, 0, off // hd + h))
    return pl.pallas_call(
        functools.partial(_ctx_attn_kernel, scale=hd ** -0.5),
        grid=(b, heads),
        in_specs=[pl.BlockSpec((1, n, hd), col(off_q)),
                  pl.BlockSpec((1, n, hd), col(off_k)),
                  pl.BlockSpec((1, n, hd), col(off_v))],
        out_specs=pl.BlockSpec((1, n, hd), lambda bi, h: (bi, 0, h)),
        out_shape=jax.ShapeDtypeStruct((b, n, heads * hd), BF16),
        compiler_params=_params("parallel", "parallel"),
    )(pc, pc, pc)


def _fn1_kernel(u_ref, cs_ref, o_ref, *, groups, gd):
    for g in range(groups):
        sl = slice(g * gd, (g + 1) * gd)
        y = _dot(u_ref[0, :, sl], cs_ref[...])
        o_ref[0, 0, :, sl] = y[:, :gd].astype(BF16)
        o_ref[0, 1, :, sl] = y[:, gd:].astype(BF16)


def _fn2_kernel(d_ref, y_ref, o_ref):
    o_ref[0] = _dot(d_ref[...], y_ref[0]).astype(BF16)


def _dft_tables(s, gd):
    def angles(n):
        k = jnp.arange(n, dtype=jnp.int32)
        return ((k[:, None] * k[None, :]) % n).astype(F32) * (2.0 * math.pi / n)

    ang_c = angles(gd)
    cs = jnp.concatenate([jnp.cos(ang_c), jnp.sin(ang_c)], axis=1) * (1.0 / math.sqrt(s * gd))
    ang_s = angles(s)
    dm = jnp.concatenate([jnp.cos(ang_s), -jnp.sin(ang_s)], axis=1)
    return cs.astype(BF16), dm.astype(BF16)


def _fourier(p3, cs, dm, *, off_u, width):
    b, s, _ = p3.shape
    gd = width // FN_GROUPS
    ts = _tile(s, 1024, 512, 256, 128)
    ycat = pl.pallas_call(
        functools.partial(_fn1_kernel, groups=FN_GROUPS, gd=gd),
        grid=(b, s // ts),
        in_specs=[pl.BlockSpec((1, ts, width), lambda bi, i: (bi, i, off_u // width)),
                  pl.BlockSpec((gd, 2 * gd), lambda bi, i: (0, 0))],
        out_specs=pl.BlockSpec((1, 2, ts, width), lambda bi, i: (bi, 0, i, 0)),
        out_shape=jax.ShapeDtypeStruct((b, 2, s, width), BF16),
        compiler_params=_params("parallel", "parallel"),
    )(p3, cs)
    ycat = ycat.reshape(b, 2 * s, width)
    tm = _tile(s, 1024, 512, 256, 128)
    tn = _tile(width, 512, 256, 128)
    return pl.pallas_call(
        _fn2_kernel,
        grid=(s // tm, b, width // tn),
        in_specs=[pl.BlockSpec((tm, 2 * s), lambda i, bi, j: (i, 0)),
                  pl.BlockSpec((1, 2 * s, tn), lambda i, bi, j: (bi, 0, j))],
        out_specs=pl.BlockSpec((1, tm, tn), lambda i, bi, j: (bi, i, j)),
        out_shape=jax.ShapeDtypeStruct((b, s, width), BF16),
        compiler_params=_params("parallel", "parallel", "parallel"),
    )(dm, ycat)


def _conv_kernel(x_ref, w_ref, b_ref, o_ref):
    x = x_ref[0].astype(F32)
    n = x.shape[0]
    taps = w_ref.shape[0]
    pad = taps // 2
    t = lax.broadcasted_iota(jnp.int32, x.shape, 0)
    acc = b_ref[...] + w_ref[pad:pad + 1, :] * x
    for k in range(taps):
        sh = pad - k
        if sh == 0:
            continue
        shifted = pltpu.roll(x, sh % n, 0)
        valid = (t >= sh) if sh > 0 else (t < n + sh)
        acc = acc + w_ref[k:k + 1, :] * jnp.where(valid, shifted, 0.0)
    o_ref[0] = _silu(acc).astype(BF16)


def _conv(p3, conv_w, conv_b, *, off_xbc):
    b, n, _ = p3.shape
    taps, cd = conv_w.shape
    tc = _tile(cd, 512, 256, 128)
    assert off_xbc % tc == 0
    return pl.pallas_call(
        _conv_kernel,
        grid=(b, cd // tc),
        in_specs=[pl.BlockSpec((1, n, tc), lambda bi, j: (bi, 0, off_xbc // tc + j)),
                  pl.BlockSpec((taps, tc), lambda bi, j: (0, j)),
                  pl.BlockSpec((1, tc), lambda bi, j: (0, j))],
        out_specs=pl.BlockSpec((1, n, tc), lambda bi, j: (bi, 0, j)),
        out_shape=jax.ShapeDtypeStruct((b, n, cd), BF16),
        compiler_params=_params("parallel", "parallel"),
    )(p3, conv_w, conv_b.reshape(1, cd))


def _ssd_kernel(xf_ref, xb_ref, dtf_ref, dtb_ref, dttf_ref, dttb_ref, brow_ref, bcol_ref,
                arow_ref, acol_ref, h0_ref, yf_ref, yb_ref, hout_ref, st_scr, yd_scr,
                *, heads, inner, groups, nstate):
    c = pl.program_id(1)
    nc = pl.num_programs(1)
    q = xf_ref.shape[1]
    hp = inner // heads
    gw = inner // groups
    hpg = heads // groups

    @pl.when(c == 0)
    def _():
        st_scr[...] = h0_ref[0]

    li = lax.broadcasted_iota(jnp.int32, (q, q), 0)
    si = lax.broadcasted_iota(jnp.int32, (q, q), 1)
    lower = li >= si
    upper = li <= si
    eh = lax.broadcasted_iota(jnp.int32, (heads, inner), 0)
    ec = lax.broadcasted_iota(jnp.int32, (heads, inner), 1)
    expand = jnp.where((ec >= eh * hp) & (ec < (eh + 1) * hp), 1.0, 0.0).astype(BF16)

    for d, (x_ref, dt_ref, dtt_ref, y_ref) in enumerate(
            ((xf_ref, dtf_ref, dttf_ref, yf_ref), (xb_ref, dtb_ref, dttb_ref, yb_ref))):
        hs = slice(d * heads, (d + 1) * heads)
        mask = lower if d == 0 else upper
        tri = jnp.where(mask, 1.0, 0.0).astype(BF16)
        tri_t = jnp.where(upper if d == 0 else lower, 1.0, 0.0).astype(BF16)
        dt = _softplus(dt_ref[0][:, hs] + brow_ref[:, hs])
        dtt = _softplus(dtt_ref[hs, :] + bcol_ref[hs, :])
        acum = _dot_sel_r(tri, dt * arow_ref[:, hs])
        acum_t = _dot_sel_l(dtt * acol_ref[hs, :], tri_t)
        total = acum[q - 1:q, :] if d == 0 else acum[0:1, :]
        dt_e = _dot_sel_l(dt, expand)
        dec_e = _dot_sel_l(jnp.exp(total - acum), expand)
        esc_e = _dot_sel_l(jnp.exp(acum), expand)
        tot_e = _dot_sel_l(jnp.broadcast_to(jnp.exp(total), (8, heads)), expand)[0:1, :]
        xd = x_ref[0, :, 0:inner].astype(F32) * dt_e
        xd_b = xd.astype(BF16)
        xdd_b = (xd * dec_e).astype(BF16)
        for g in range(groups):
            gs = slice(g * gw, (g + 1) * gw)
            bg = x_ref[0, :, inner + g * nstate:inner + (g + 1) * nstate]
            cg = x_ref[0, :, inner + (groups + g) * nstate:inner + (groups + g + 1) * nstate]
            cb = _dot_nt(cg, bg)
            st_prev = st_scr[d, :, gs]
            y_off = _dot(cg, st_prev.astype(BF16))
            for r in range(hpg):
                h = g * hpg + r
                seg = acum[:, h:h + 1] - acum_t[h:h + 1, :]
                lmat = jnp.exp(jnp.where(mask, seg, MASK_VALUE))
                m = (cb * lmat).astype(BF16)
                yd_scr[:, h * hp:(h + 1) * hp] = _dot(m, xd_b[:, h * hp:(h + 1) * hp])
            y_ref[0, :, gs] = (yd_scr[:, gs] + y_off * esc_e[:, gs]).astype(BF16)
            st_scr[d, :, gs] = st_prev * tot_e[:, gs] + _dot_tn(bg, xdd_b[:, gs])

    @pl.when(c == nc - 1)
    def _():
        hout_ref[0] = st_scr[...]


def _ssd(xc, dt, dtt, dt_bias, a_neg, h0, *, heads, inner):
    b, n, cd = xc.shape
    q = SSM_CHUNK
    assert n % q == 0
    nc = n // q
    groups = SSM_GROUPS
    nstate = (cd - inner) // (2 * groups)
    nh2 = 2 * heads
    brow = dt_bias.reshape(1, nh2)
    bcol = dt_bias.reshape(nh2, 1)
    arow = a_neg.reshape(1, nh2)
    acol = a_neg.reshape(nh2, 1)
    dt3 = dt.reshape(b, n, nh2)
    fwd3 = lambda bi, c: (bi, c, 0)
    bwd3 = lambda bi, c: (bi, nc - 1 - c, 0)
    const2 = lambda bi, c: (0, 0)
    kern = functools.partial(_ssd_kernel, heads=heads, inner=inner, groups=groups, nstate=nstate)
    return pl.pallas_call(
        kern,
        grid=(b, nc),
        in_specs=[pl.BlockSpec((1, q, cd), fwd3),
                  pl.BlockSpec((1, q, cd), bwd3),
                  pl.BlockSpec((1, q, nh2), fwd3),
                  pl.BlockSpec((1, q, nh2), bwd3),
                  pl.BlockSpec((nh2, q), lambda bi, c: (0, bi * nc + c)),
                  pl.BlockSpec((nh2, q), lambda bi, c: (0, bi * nc + nc - 1 - c)),
                  pl.BlockSpec((1, nh2), const2),
                  pl.BlockSpec((nh2, 1), const2),
                  pl.BlockSpec((1, nh2), const2),
                  pl.BlockSpec((nh2, 1), const2),
                  pl.BlockSpec((1, 2, nstate, inner), lambda bi, c: (bi, 0, 0, 0))],
        out_specs=[pl.BlockSpec((1, q, inner), fwd3),
                   pl.BlockSpec((1, q, inner), bwd3),
                   pl.BlockSpec((1, 2, nstate, inner), lambda bi, c: (bi, 0, 0, 0))],
        out_shape=[jax.ShapeDtypeStruct((b, n, inner), BF16),
                   jax.ShapeDtypeStruct((b, n, inner), BF16),
                   jax.ShapeDtypeStruct((b, 2, nstate, inner), F32)],
        scratch_shapes=[pltpu.VMEM((2, nstate, inner), F32), pltpu.VMEM((q, inner), F32)],
        compiler_params=_params("parallel", "arbitrary"),
    )(xc, xc, dt3, dt3, dtt, dtt, brow, bcol, arow, acol, h0)


def _merge_kernel(ona_ref, ofn_ref, yf_ref, yb_ref, xs_ref, z_ref, gna_ref, gfn_ref, gss_ref,
                  dsk_ref, nw_ref, w1_ref, w2_ref, w3_ref, o_ref, oss_scr, *, groups):
    @pl.when(pl.program_id(1) == 0)
    def _():
        y = yf_ref[...].astype(F32) + yb_ref[...].astype(F32) + dsk_ref[...] * xs_ref[...].astype(F32)
        gy = y * _silu(z_ref[...].astype(F32))
        gw = gy.shape[1] // groups
        for g in range(groups):
            sl = slice(g * gw, (g + 1) * gw)
            v = gy[:, sl]
            ms = jnp.mean(v * v, axis=-1, keepdims=True)
            oss_scr[:, sl] = (v * lax.rsqrt(ms + RMS_EPS) * nw_ref[:, sl]).astype(BF16)

    gate = lambda ref: _sigmoid(ref[...].astype(F32))
    m = (gate(gna_ref) * _dot(ona_ref[...], w1_ref[...])
         + gate(gfn_ref) * _dot(ofn_ref[...], w2_ref[...])
         + gate(gss_ref) * _dot(oss_scr[...], w3_ref[...]))
    o_ref[...] = m.astype(BF16)


def _merge(o_na, o_fn, y_f, y_b, xc2, p2, d_e, norm_w, w1, w2, w3, *, off_z, off_g, inner):
    m, na = o_na.shape
    fn = o_fn.shape[1]
    d = w1.shape[1]
    tm = _tile(m, 512, 256, 128)
    tn = _tile(d, 512, 256, 128)
    assert off_z % inner == 0 and off_g % tn == 0 and d % tn == 0
    row = lambda w: pl.BlockSpec((tm, w), lambda i, j: (i, 0))
    gate = lambda k: pl.BlockSpec((tm, tn), lambda i, j: (i, (off_g + k * d) // tn + j))
    return pl.pallas_call(
        functools.partial(_merge_kernel, groups=SSM_GROUPS),
        grid=(m // tm, d // tn),
        in_specs=[row(na), row(fn), row(inner), row(inner), row(inner),
                  pl.BlockSpec((tm, inner), lambda i, j: (i, off_z // inner)),
                  gate(0), gate(1), gate(2),
                  pl.BlockSpec((1, inner), lambda i, j: (0, 0)),
                  pl.BlockSpec((1, inner), lambda i, j: (0, 0)),
                  pl.BlockSpec((na, tn), lambda i, j: (0, j)),
                  pl.BlockSpec((fn, tn), lambda i, j: (0, j)),
                  pl.BlockSpec((inner, tn), lambda i, j: (0, j))],
        out_specs=pl.BlockSpec((tm, tn), lambda i, j: (i, j)),
        out_shape=jax.ShapeDtypeStruct((m, d), BF16),
        scratch_shapes=[pltpu.VMEM((tm, inner), BF16)],
        compiler_params=_params("parallel", "arbitrary"),
    )(o_na, o_fn, y_f, y_b, xc2, p2, p2, p2, p2, d_e, norm_w, w1, w2, w3)


def _outproj_kernel(x_ref, m_ref, g_ref, w_ref, lg_ref, lb_ref, o_ref, *, alpha):
    mix = _dot(m_ref[...], w_ref[...])
    y = alpha * x_ref[...] + g_ref[0] * mix
    o_ref[...] = _ln(y) * lg_ref[...] + lb_ref[...]


def _outproj(x2d, m2d, gate, rows_per_mod, w_out, ln_g, ln_b, alpha):
    m, d = x2d.shape
    tm = _tile(m, 512, 256, 128)
    assert rows_per_mod % tm == 0
    vec = pl.BlockSpec((1, d), lambda i: (0, 0))
    return pl.pallas_call(
        functools.partial(_outproj_kernel, alpha=alpha),
        grid=(m // tm,),
        in_specs=[pl.BlockSpec((tm, d), lambda i: (i, 0)),
                  pl.BlockSpec((tm, d), lambda i: (i, 0)),
                  pl.BlockSpec((1, 1, d), lambda i: (i * tm // rows_per_mod, 0, 0)),
                  pl.BlockSpec((d, d), lambda i: (0, 0)),
                  vec, vec],
        out_specs=pl.BlockSpec((tm, d), lambda i: (i, 0)),
        out_shape=jax.ShapeDtypeStruct((m, d), F32),
        compiler_params=_params("parallel"),
    )(x2d, m2d, gate, w_out, ln_g.reshape(1, d), ln_b.reshape(1, d))


def _ffn_kernel(x_ref, sh_ref, sc_ref, g_ref, wg_ref, wu_ref, wd_ref, lg_ref, lb_ref, o_ref,
                h_scr, acc_scr, *, alpha):
    j = pl.program_id(1)

    @pl.when(j == 0)
    def _():
        h_scr[...] = (_ln(x_ref[...]) * (1.0 + sc_ref[0]) + sh_ref[0]).astype(BF16)
        acc_scr[...] = jnp.zeros_like(acc_scr)

    h = h_scr[...]
    a = (_silu(_dot(h, wg_ref[...])) * _dot(h, wu_ref[...])).astype(BF16)
    acc_scr[...] += _dot(a, wd_ref[...])

    @pl.when(j == pl.num_programs(1) - 1)
    def _():
        y = alpha * x_ref[...] + g_ref[0] * acc_scr[...]
        o_ref[...] = _ln(y) * lg_ref[...] + lb_ref[...]


def _ffn(x2d, shift, scale, gate, rows_per_mod, wg, wu, wd, ln_g, ln_b, alpha):
    m, d = x2d.shape
    hid = wg.shape[1]
    tm = _tile(m, 512, 256, 128)
    th = _tile(hid, 512, 256, 128)
    assert rows_per_mod % tm == 0
    mod = pl.BlockSpec((1, 1, d), lambda i, j: (i * tm // rows_per_mod, 0, 0))
    vec = pl.BlockSpec((1, d), lambda i, j: (0, 0))
    return pl.pallas_call(
        functools.partial(_ffn_kernel, alpha=alpha),
        grid=(m // tm, hid // th),
        in_specs=[pl.BlockSpec((tm, d), lambda i, j: (i, 0)),
                  mod, mod, mod,
                  pl.BlockSpec((d, th), lambda i, j: (0, j)),
                  pl.BlockSpec((d, th), lambda i, j: (0, j)),
                  pl.BlockSpec((th, d), lambda i, j: (j, 0)),
                  vec, vec],
        out_specs=pl.BlockSpec((tm, d), lambda i, j: (i, 0)),
        out_shape=jax.ShapeDtypeStruct((m, d), F32),
        scratch_shapes=[pltpu.VMEM((tm, d), BF16), pltpu.VMEM((tm, d), F32)],
        compiler_params=_params("parallel", "arbitrary"),
    )(x2d, shift, scale, gate, wg, wu, wd, ln_g.reshape(1, d), ln_b.reshape(1, d))


def _rope_tables(n_tokens):
    t = np.arange(n_tokens)
    half = NA_HEAD_DIM // 4
    inv = ROPE_THETA ** (-np.arange(half, dtype=np.float32) / half)
    pos = np.stack([t // GRID_W, t % GRID_W], axis=1).astype(np.float32)
    lane = np.arange(NA_HEAD_DIM)
    ang = pos[:, lane // (2 * half)] * inv[lane % half][None, :]
    sign = np.where((lane % (2 * half)) < half, -1.0, 1.0).astype(np.float32)
    return jnp.asarray(np.cos(ang), F32), jnp.asarray(np.sin(ang) * sign, F32)


def _bias_table(rpb):
    kc = NA_WIN_COLS
    col = np.arange(GRID_W)
    c_start = np.clip(col - kc // 2, 0, GRID_W - kc)
    col_in = (col[None, :] >= c_start[:, None]) & (col[None, :] < c_start[:, None] + kc)
    dc = np.clip(col[None, :] - col[:, None], -(kc - 1), kc - 1) + (kc - 1)
    t = jnp.where(col_in[None, None], rpb[:, :, dc].astype(F32), MASK_VALUE)
    slabs = [jnp.transpose(t[:, d0:d0 + NA_WIN_ROWS], (0, 2, 1, 3)).reshape(
        rpb.shape[0], GRID_W, NA_WIN_ROWS * GRID_W) for d0 in range(NA_WIN_ROWS)]
    return jnp.stack(slabs, axis=1)


def kernel(x, c, ctx, c_ctx, w_mod, w_in, na_rpb, ssm_conv_w, ssm_conv_b, ssm_a_log, ssm_dt_bias, ssm_d, ssm_norm_w, w_br_na, w_br_fn, w_br_ssm, w_out, ln1_g, ln1_b, w_ffn_gate, w_ffn_up, w_ffn_down, ln2_g, ln2_b):
    b, s, d = x.shape
    n_ctx = ctx.shape[1]
    depth = w_mod.shape[0]
    na = w_br_na.shape[1]
    fn = w_br_fn.shape[1]
    inner = w_br_ssm.shape[1]
    conv_dim = ssm_conv_w.shape[2]
    ssm_heads = ssm_d.shape[1]
    na_heads = na_rpb.shape[1]
    assert na == na_heads * NA_HEAD_DIM and inner == ssm_heads * SSM_HEAD_DIM
    alpha = (2.0 * depth) ** 0.25

    c_u = 3 * na + fn
    c_z = c_u + conv_dim
    c_dt = c_z + inner
    c_g = c_dt + 2 * ssm_heads
    off_q, off_k, off_v, off_u = 0, na, 2 * na, 3 * na
    off_z = c_u
    off_g = off_z + inner
    off_xbc = off_g + 3 * d
    assert off_u % fn == 0 and off_z % inner == 0

    rows = b + 1
    rows_pad = -(-rows // MOD_ROWS_ALIGN) * MOD_ROWS_ALIGN
    c_rows = jnp.concatenate([c, c_ctx[None, :], jnp.zeros((rows_pad - rows, d), F32)], axis=0)
    mods = _mods(c_rows, w_mod)

    cos_t, sin_t = _rope_tables(s)
    cs_x, dm_x = _dft_tables(s, fn // FN_GROUPS)
    cs_c, dm_c = _dft_tables(n_ctx, fn // FN_GROUPS)
    nstate = (conv_dim - inner) // (2 * SSM_GROUPS)
    zero_state = jnp.zeros((b, 2, nstate, inner), F32)

    x2 = x.reshape(b * s, d)
    c2 = ctx.reshape(b * n_ctx, d)
    for l in range(depth):
        with_ctx = l < depth - 1
        wl = w_in[l]
        w_main = jnp.concatenate([wl[:, :c_u], wl[:, c_z:c_dt], wl[:, c_g:], wl[:, c_u:c_z]], axis=1).astype(BF16)
        w_dt = wl[:, c_dt:c_g].astype(BF16)
        w_dtt = w_dt.T
        mx = [v.reshape(b, 1, d) for v in jnp.split(mods[l, :b], 6, axis=-1)]
        mc = [v.reshape(1, 1, d) for v in jnp.split(mods[l, b:b + 1], 6, axis=-1)]
        a_neg = -jnp.exp(ssm_a_log[l].astype(F32))
        d_e = jnp.repeat(ssm_d[l].astype(F32), SSM_HEAD_DIM).reshape(1, inner)
        norm_w = ssm_norm_w[l].astype(F32).reshape(1, inner)
        w1, w2, w3 = w_br_na[l].astype(BF16), w_br_fn[l].astype(BF16), w_br_ssm[l].astype(BF16)
        wo = w_out[l].astype(BF16)
        wg, wu, wd = w_ffn_gate[l].astype(BF16), w_ffn_up[l].astype(BF16), w_ffn_down[l].astype(BF16)
        bias_t = _bias_table(na_rpb[l])

        px, dtx, dttx = _inproj(x2, mx[0], mx[1], s, w_main, w_dt, w_dtt)
        pc, dtc, dttc = _inproj(c2, mc[0], mc[1], b * n_ctx, w_main, w_dt, w_dtt)
        px3 = px.reshape(b, s, -1)
        pc3 = pc.reshape(b, n_ctx, -1)

        o_na = _na_latent(px3, pc3, cos_t, sin_t, bias_t, heads=na_heads, off_q=off_q, off_k=off_k, off_v=off_v)
        o_fn = _fourier(px3, cs_x, dm_x, off_u=off_u, width=fn)
        xc_c = _conv(pc3, ssm_conv_w[l], ssm_conv_b[l], off_xbc=off_xbc)
        yf_c, yb_c, h_ctx = _ssd(xc_c, dtc, dttc, ssm_dt_bias[l], a_neg, zero_state, heads=ssm_heads, inner=inner)
        xc_x = _conv(px3, ssm_conv_w[l], ssm_conv_b[l], off_xbc=off_xbc)
        yf_x, yb_x, _ = _ssd(xc_x, dtx, dttx, ssm_dt_bias[l], a_neg, h_ctx, heads=ssm_heads, inner=inner)

        flat = lambda t: t.reshape(-1, t.shape[-1])
        m_x = _merge(flat(o_na), flat(o_fn), flat(yf_x), flat(yb_x), flat(xc_x), px, d_e, norm_w,
                     w1, w2, w3, off_z=off_z, off_g=off_g, inner=inner)
        x2 = _outproj(x2, m_x, mx[2], s, wo, ln1_g[l], ln1_b[l], alpha)
        x2 = _ffn(x2, mx[3], mx[4], mx[5], s, wg, wu, wd, ln2_g[l], ln2_b[l], alpha)
        if with_ctx:
            o_na_c = _ctx_attn(pc3, heads=na_heads, off_q=off_q, off_k=off_k, off_v=off_v)
            o_fn_c = _fourier(pc3, cs_c, dm_c, off_u=off_u, width=fn)
            m_c = _merge(flat(o_na_c), flat(o_fn_c), flat(yf_c), flat(yb_c), flat(xc_c), pc, d_e, norm_w,
                         w1, w2, w3, off_z=off_z, off_g=off_g, inner=inner)
            c2 = _outproj(c2, m_c, mc[2], b * n_ctx, wo, ln1_g[l], ln1_b[l], alpha)
            c2 = _ffn(c2, mc[3], mc[4], mc[5], b * n_ctx, wg, wu, wd, ln2_g[l], ln2_b[l], alpha)
    return x2.reshape(b, s, d)
```

```python
import functools
import math

import jax
import jax.numpy as jnp
import numpy as np
from jax import lax
from jax.experimental import pallas as pl
from jax.experimental.pallas import tpu as pltpu

F32 = jnp.float32
BF16 = jnp.bfloat16

GRID_W = 64
NA_WIN_ROWS = 8
NA_WIN_COLS = 16
NA_HEAD_DIM = 128
ROPE_THETA = 10000.0
FN_GROUPS = 4
SSM_GROUPS = 4
SSM_HEAD_DIM = 64
SSM_CHUNK = 128
LN_EPS = 1e-6
RMS_EPS = 1e-5
MASK_VALUE = -1e30
LOG2E = 1.0 / math.log(2.0)

NA_QROWS = 4
NA_KROWS = NA_WIN_ROWS + NA_QROWS

V7X_VMEM_BYTES = 64 * 1024 * 1024
VMEM_LIMIT_BYTES = V7X_VMEM_BYTES - 8 * 1024 * 1024
V7X_LANES = 128
V7X_SUBLANES = 8
MOD_ROWS_ALIGN = 16


def _params(*sem):
    return pltpu.CompilerParams(dimension_semantics=sem, vmem_limit_bytes=VMEM_LIMIT_BYTES)


def _tile(n, *candidates):
    for c in candidates:
        if n % c == 0:
            return c
    return n


def _sigmoid(x):
    return 1.0 / (1.0 + jnp.exp(-x))


def _silu(x):
    return x * _sigmoid(x)


def _softplus(x):
    return jnp.maximum(x, 0.0) + jnp.log(1.0 + jnp.exp(-jnp.abs(x)))


def _ln(x):
    mu = jnp.mean(x, axis=-1, keepdims=True)
    xc = x - mu
    var = jnp.mean(xc * xc, axis=-1, keepdims=True)
    return xc * lax.rsqrt(var + LN_EPS)


def _dot(a, b):
    return jnp.dot(a, b, preferred_element_type=F32)


def _dot_nt(a, b):
    return lax.dot_general(a, b, (((1,), (1,)), ((), ())), preferred_element_type=F32)


def _split3(a):
    hi = a.astype(BF16)
    r = a - hi.astype(F32)
    mid = r.astype(BF16)
    lo = (r - mid.astype(F32)).astype(BF16)
    return hi, mid, lo


def _dot_sel_l(a, sel):
    hi, mid, lo = _split3(a)
    return _dot(hi, sel) + _dot(mid, sel) + _dot(lo, sel)


def _dot_sel_nt(sel, a):
    hi, mid, lo = _split3(a)
    return _dot_nt(sel, hi) + _dot_nt(sel, mid) + _dot_nt(sel, lo)


def _mods_kernel(c_ref, w_ref, o_ref):
    a = _silu(c_ref[...]).astype(BF16)
    o_ref[0] = _dot(a, w_ref[0].astype(BF16))


def _mods(c_rows, w_mod):
    depth, d, n = w_mod.shape
    r = c_rows.shape[0]
    tn = _tile(n, 1024, 512, 256, 128)
    return pl.pallas_call(
        _mods_kernel,
        grid=(depth, n // tn),
        in_specs=[pl.BlockSpec((r, d), lambda l, j: (0, 0)),
                  pl.BlockSpec((1, d, tn), lambda l, j: (l, 0, j))],
        out_specs=pl.BlockSpec((1, r, tn), lambda l, j: (l, 0, j)),
        out_shape=jax.ShapeDtypeStruct((depth, r, n), F32),
        compiler_params=_params("parallel", "parallel"),
    )(c_rows, w_mod)


def _inproj_kernel(x_ref, sh_ref, sc_ref, w_ref, wdtt_ref, o_ref, dtt_ref, h_scr):
    @pl.when(pl.program_id(1) == 0)
    def _():
        h = (_ln(x_ref[...]) * (1.0 + sc_ref[0]) + sh_ref[0]).astype(BF16)
        h_scr[...] = h
        dtt_ref[...] = _dot_nt(wdtt_ref[...], h)

    o_ref[...] = _dot(h_scr[...], w_ref[...]).astype(BF16)


def _inproj(x2d, shift, scale, rows_per_mod, w_main, w_dtt):
    m, d = x2d.shape
    n = w_main.shape[1]
    ndt = w_dtt.shape[0]
    tm = _tile(m, 1024, 512, 256, 128)
    tn = _tile(n, 1024, 512, 256, 128)
    assert rows_per_mod % tm == 0
    mod_spec = pl.BlockSpec((1, 1, d), lambda i, j: (i * tm // rows_per_mod, 0, 0))
    return pl.pallas_call(
        _inproj_kernel,
        grid=(m // tm, n // tn),
        in_specs=[pl.BlockSpec((tm, d), lambda i, j: (i, 0)),
                  mod_spec, mod_spec,
                  pl.BlockSpec((d, tn), lambda i, j: (0, j)),
                  pl.BlockSpec((ndt, d), lambda i, j: (0, 0))],
        out_specs=[pl.BlockSpec((tm, tn), lambda i, j: (i, j)),
                   pl.BlockSpec((ndt, tm), lambda i, j: (0, i))],
        out_shape=[jax.ShapeDtypeStruct((m, n), BF16),
                   jax.ShapeDtypeStruct((ndt, m), F32)],
        scratch_shapes=[pltpu.VMEM((tm, d), BF16)],
        compiler_params=_params("parallel", "arbitrary"),
    )(x2d, shift, scale, w_main, w_dtt)


def _na_plan(rows):
    assert rows % NA_QROWS == 0 and rows >= NA_KROWS
    blocks, cases = [], []
    for r0 in range(0, rows, NA_QROWS):
        start = min(max(r0 - NA_WIN_ROWS // 2, 0), rows - NA_KROWS)
        rs = [min(max(r - NA_WIN_ROWS // 2, 0), rows - NA_WIN_ROWS) for r in range(r0, r0 + NA_QROWS)]
        case = (tuple(v - start for v in rs), start - r0)
        if case not in cases:
            cases.append(case)
        blocks.append((r0, start, cases.index(case)))
    return blocks, cases


def _na_kernel(q_ref, k_ref, v_ref, kc_ref, vc_ref, cos_ref, sin_ref, bias_ref, o_ref,
               qr_scr, qs_scr, kr_scr, *, blocks, scale):
    s, hd = qr_scr.shape
    half = hd // 4
    src = lax.broadcasted_iota(jnp.int32, (hd, hd), 0)
    dst = lax.broadcasted_iota(jnp.int32, (hd, hd), 1)
    partner = jnp.where((dst % (2 * half)) < half, dst + half, dst - half)
    perm = jnp.where(src == partner, 1.0, 0.0).astype(BF16)

    def rope(t):
        return t.astype(F32) * cos_ref[...] + _dot(t, perm) * sin_ref[...]

    qscale = scale * LOG2E
    qr_scr[...] = (rope(q_ref[0]) * qscale).astype(BF16)
    qs_scr[...] = (q_ref[0].astype(F32) * qscale).astype(BF16)
    kr_scr[...] = rope(k_ref[0]).astype(BF16)
    kc = kc_ref[0]
    vc = vc_ref[0]
    nq = NA_QROWS * GRID_W
    nk = NA_KROWS * GRID_W
    for r0, start, case in blocks:
        qs = slice(r0 * GRID_W, r0 * GRID_W + nq)
        ks = slice(start * GRID_W, start * GRID_W + nk)
        s_lat = _dot_nt(qr_scr[qs, :], kr_scr[ks, :]) + bias_ref[0, case]
        s_ctx = _dot_nt(qs_scr[qs, :], kc)
        mx = jnp.maximum(jnp.max(s_lat, axis=-1, keepdims=True), jnp.max(s_ctx, axis=-1, keepdims=True))
        p_lat = jnp.exp2(s_lat - mx)
        p_ctx = jnp.exp2(s_ctx - mx)
        den = jnp.sum(p_lat, axis=-1, keepdims=True) + jnp.sum(p_ctx, axis=-1, keepdims=True)
        p = jnp.concatenate([p_lat.astype(BF16), p_ctx.astype(BF16)], axis=1)
        vals = jnp.concatenate([v_ref[0, ks, :], vc], axis=0)
        o_ref[0, qs, :] = (_dot(p, vals) * (1.0 / den)).astype(BF16)


def _na_latent(px, pc, cos_t, sin_t, bias_t, blocks, *, heads, off_q, off_k, off_v):
    b, s, _ = px.shape
    n_ctx = pc.shape[1]
    hd = NA_HEAD_DIM
    col = lambda off: (lambda h, bi: (bi, 0, off // hd + h))
    kern = functools.partial(_na_kernel, blocks=blocks, scale=hd ** -0.5)
    return pl.pallas_call(
        kern,
        grid=(heads, b),
        in_specs=[pl.BlockSpec((1, s, hd), col(off_q)),
                  pl.BlockSpec((1, s, hd), col(off_k)),
                  pl.BlockSpec((1, s, hd), col(off_v)),
                  pl.BlockSpec((1, n_ctx, hd), col(off_k)),
                  pl.BlockSpec((1, n_ctx, hd), col(off_v)),
                  pl.BlockSpec((s, hd), lambda h, bi: (0, 0)),
                  pl.BlockSpec((s, hd), lambda h, bi: (0, 0)),
                  pl.BlockSpec((1,) + bias_t.shape[1:], lambda h, bi: (h, 0, 0, 0))],
        out_specs=pl.BlockSpec((1, s, hd), lambda h, bi: (bi, 0, h)),
        out_shape=jax.ShapeDtypeStruct((b, s, heads * hd), BF16),
        scratch_shapes=[pltpu.VMEM((s, hd), BF16)] * 3,
        compiler_params=_params("parallel", "parallel"),
    )(px, px, px, pc, pc, cos_t, sin_t, bias_t)


def _ctx_attn_kernel(q_ref, k_ref, v_ref, o_ref, *, scale):
    s = _dot_nt(q_ref[0], k_ref[0]) * scale
    p = jnp.exp(s - jnp.max(s, axis=-1, keepdims=True))
    den = jnp.sum(p, axis=-1, keepdims=True)
    o_ref[0] = (_dot(p.astype(BF16), v_ref[0]) / den).astype(BF16)


def _ctx_attn(pc, *, heads, off_q, off_k, off_v):
    b, n, _ = pc.shape
    hd = NA_HEAD_DIM
    col = lambda off: (lambda bi, h: (bi, 0, off // hd + h))
    return pl.pallas_call(
        functools.partial(_ctx_attn_kernel, scale=hd ** -0.5),
        grid=(b, heads),
        in_specs=[pl.BlockSpec((1, n, hd), col(off_q)),
                  pl.BlockSpec((1, n, hd), col(off_k)),
                  pl.BlockSpec((1, n, hd), col(off_v))],
        out_specs=pl.BlockSpec((1, n, hd), lambda bi, h: (bi, 0, h)),
        out_shape=jax.ShapeDtypeStruct((b, n, heads * hd), BF16),
        compiler_params=_params("parallel", "parallel"),
    )(pc, pc, pc)


def _fn1_kernel(u_ref, cs_ref, o_ref, *, groups, gd):
    for g in range(groups):
        sl = slice(g * gd, (g + 1) * gd)
        y = _dot(u_ref[0, :, sl], cs_ref[...])
        o_ref[0, 0, :, sl] = y[:, :gd].astype(BF16)
        o_ref[0, 1, :, sl] = y[:, gd:].astype(BF16)


def _fn2_kernel(d_ref, y_ref, o_ref):
    o_ref[0] = _dot(d_ref[...], y_ref[0]).astype(BF16)


def _dft_tables(s, gd):
    def angles(n):
        k = jnp.arange(n, dtype=jnp.int32)
        return ((k[:, None] * k[None, :]) % n).astype(F32) * (2.0 * math.pi / n)

    ang_c = angles(gd)
    cs = jnp.concatenate([jnp.cos(ang_c), jnp.sin(ang_c)], axis=1) * (1.0 / math.sqrt(s * gd))
    ang_s = angles(s)
    dm = jnp.concatenate([jnp.cos(ang_s), -jnp.sin(ang_s)], axis=1)
    return cs.astype(BF16), dm.astype(BF16)


def _fourier(p3, cs, dm, *, off_u, width):
    b, s, _ = p3.shape
    gd = width // FN_GROUPS
    ts = _tile(s, 1024, 512, 256, 128)
    ycat = pl.pallas_call(
        functools.partial(_fn1_kernel, groups=FN_GROUPS, gd=gd),
        grid=(b, s // ts),
        in_specs=[pl.BlockSpec((1, ts, width), lambda bi, i: (bi, i, off_u // width)),
                  pl.BlockSpec((gd, 2 * gd), lambda bi, i: (0, 0))],
        out_specs=pl.BlockSpec((1, 2, ts, width), lambda bi, i: (bi, 0, i, 0)),
        out_shape=jax.ShapeDtypeStruct((b, 2, s, width), BF16),
        compiler_params=_params("parallel", "parallel"),
    )(p3, cs)
    ycat = ycat.reshape(b, 2 * s, width)
    tm = _tile(s, 1024, 512, 256, 128)
    tn = _tile(width, 512, 256, 128)
    return pl.pallas_call(
        _fn2_kernel,
        grid=(s // tm, b, width // tn),
        in_specs=[pl.BlockSpec((tm, 2 * s), lambda i, bi, j: (i, 0)),
                  pl.BlockSpec((1, 2 * s, tn), lambda i, bi, j: (bi, 0, j))],
        out_specs=pl.BlockSpec((1, tm, tn), lambda i, bi, j: (bi, i, j)),
        out_shape=jax.ShapeDtypeStruct((b, s, width), BF16),
        compiler_params=_params("parallel", "parallel", "parallel"),
    )(dm, ycat)


def _conv_kernel(x_ref, w_ref, b_ref, o_ref):
    x = x_ref[0].astype(F32)
    n = x.shape[0]
    taps = w_ref.shape[0]
    pad = taps // 2
    halo = jnp.zeros((V7X_SUBLANES, x.shape[1]), F32)
    xp = jnp.concatenate([halo, x, halo], axis=0)
    acc = b_ref[...] + w_ref[pad:pad + 1, :] * x
    for k in range(taps):
        sh = pad - k
        if sh == 0:
            continue
        shifted = pltpu.roll(xp, sh % xp.shape[0], 0)[V7X_SUBLANES:V7X_SUBLANES + n]
        acc = acc + w_ref[k:k + 1, :] * shifted
    o_ref[0] = _silu(acc).astype(BF16)


def _conv(p3, conv_w, conv_b, *, off_xbc):
    b, n, _ = p3.shape
    taps, cd = conv_w.shape
    tc = _tile(cd, 512, 256, 128)
    assert off_xbc % tc == 0
    return pl.pallas_call(
        _conv_kernel,
        grid=(b, cd // tc),
        in_specs=[pl.BlockSpec((1, n, tc), lambda bi, j: (bi, 0, off_xbc // tc + j)),
                  pl.BlockSpec((taps, tc), lambda bi, j: (0, j)),
                  pl.BlockSpec((1, tc), lambda bi, j: (0, j))],
        out_specs=pl.BlockSpec((1, n, tc), lambda bi, j: (bi, 0, j)),
        out_shape=jax.ShapeDtypeStruct((b, n, cd), BF16),
        compiler_params=_params("parallel", "parallel"),
    )(p3, conv_w, conv_b.reshape(1, cd))


def _ssd_kernel(xf_ref, xb_ref, dttf_ref, dttb_ref, bcol_ref, acol_ref, dsk_ref, h0_ref,
                yf_ref, yb_ref, hout_ref, st_scr, *, heads, inner, groups, nstate):
    c = pl.program_id(1)
    nc = pl.num_programs(1)
    q = xf_ref.shape[1]
    hp = inner // heads
    hpg = heads // groups
    gw = inner // groups
    lanes = V7X_LANES
    hpt = lanes // hp
    tpg = hpg // hpt

    @pl.when(c == 0)
    def _():
        st_scr[...] = h0_ref[0]

    li = lax.broadcasted_iota(jnp.int32, (q, q), 0)
    si = lax.broadcasted_iota(jnp.int32, (q, q), 1)
    lower = li >= si
    upper = li <= si
    lane_head = lax.broadcasted_iota(jnp.int32, (q, lanes), 1) // hp

    for d, (x_ref, dtt_ref, y_ref) in enumerate(((xf_ref, dttf_ref, yf_ref), (xb_ref, dttb_ref, yb_ref))):
        hs = slice(d * heads, (d + 1) * heads)
        mask = lower if d == 0 else upper
        tri = jnp.where(mask, 1.0, 0.0).astype(BF16)
        dtt = _softplus(dtt_ref[hs, :] + bcol_ref[hs, :])
        a_t = dtt * (acol_ref[hs, :] * LOG2E)
        acum = _dot_sel_nt(tri, a_t)
        acum_t = _dot_sel_l(a_t, jnp.where(upper if d == 0 else lower, 1.0, 0.0).astype(BF16))
        total = acum[q - 1:q, :] if d == 0 else acum[0:1, :]
        total_t = acum_t[:, q - 1:q] if d == 0 else acum_t[:, 0:1]
        w_t = dtt * jnp.exp2(total_t - acum_t)
        row_t = acum_t - jnp.log2(dtt)
        tot = jnp.exp2(total)
        for g in range(groups):
            bg = x_ref[0, :, inner + g * nstate:inner + (g + 1) * nstate]
            cg = x_ref[0, :, inner + (groups + g) * nstate:inner + (groups + g + 1) * nstate]
            cb = _dot_nt(cg, bg)
            bg_t = bg.astype(F32).T
            y_off = _dot(cg, st_scr[d, :, g * gw:(g + 1) * gw].astype(BF16))
            for t in range(tpg):
                h0 = g * hpg + t * hpt
                ls = slice(h0 * hp, h0 * hp + lanes)
                xt = x_ref[0, :, ls]
                lhs_y, lhs_s, rhs_x = [], [], []
                esc = tot_l = None
                for j in range(hpt):
                    h = h0 + j
                    col = acum[:, h:h + 1]
                    lmat = jnp.exp2(jnp.where(mask, col - row_t[h:h + 1, :], MASK_VALUE))
                    lhs_y.append((cb * lmat).astype(BF16))
                    lhs_s.append((bg_t * w_t[h:h + 1, :]).astype(BF16))
                    own = lane_head == j
                    rhs_x.append(jnp.where(own, xt, jnp.zeros_like(xt)))
                    esc_j = jnp.exp2(jnp.broadcast_to(col, (q, lanes)))
                    tot_j = jnp.broadcast_to(tot[:, h:h + 1], (1, lanes))
                    esc = esc_j if j == 0 else jnp.where(own, esc_j, esc)
                    tot_l = tot_j if j == 0 else jnp.where(own[0:1], tot_j, tot_l)
                x_bd = jnp.concatenate(rhs_x, axis=0)
                y = _dot(jnp.concatenate(lhs_y, axis=1), x_bd) + esc * y_off[:, t * lanes:(t + 1) * lanes]
                if d == 0:
                    y = y + dsk_ref[:, ls] * xt.astype(F32)
                y_ref[0, :, ls] = y.astype(BF16)
                st_scr[d, :, ls] = st_scr[d, :, ls] * tot_l + _dot(jnp.concatenate(lhs_s, axis=1), x_bd)

    @pl.when(c == nc - 1)
    def _():
        hout_ref[0] = st_scr[...]


def _ssd(xc, dtt, dt_bias, a_neg, d_e, h0, *, heads, inner):
    b, n, cd = xc.shape
    q = SSM_CHUNK
    assert n % q == 0
    nc = n // q
    groups = SSM_GROUPS
    nstate = (cd - inner) // (2 * groups)
    hp = inner // heads
    assert V7X_LANES % hp == 0 and (heads // groups) % (V7X_LANES // hp) == 0 and nstate <= q
    nh2 = 2 * heads
    bcol = dt_bias.reshape(nh2, 1)
    acol = a_neg.reshape(nh2, 1)
    fwd3 = lambda bi, c: (bi, c, 0)
    bwd3 = lambda bi, c: (bi, nc - 1 - c, 0)
    const2 = lambda bi, c: (0, 0)
    kern = functools.partial(_ssd_kernel, heads=heads, inner=inner, groups=groups, nstate=nstate)
    return pl.pallas_call(
        kern,
        grid=(b, nc),
        in_specs=[pl.BlockSpec((1, q, cd), fwd3),
                  pl.BlockSpec((1, q, cd), bwd3),
                  pl.BlockSpec((nh2, q), lambda bi, c: (0, bi * nc + c)),
                  pl.BlockSpec((nh2, q), lambda bi, c: (0, bi * nc + nc - 1 - c)),
                  pl.BlockSpec((nh2, 1), const2),
                  pl.BlockSpec((nh2, 1), const2),
                  pl.BlockSpec((1, inner), const2),
                  pl.BlockSpec((1, 2, nstate, inner), lambda bi, c: (bi, 0, 0, 0))],
        out_specs=[pl.BlockSpec((1, q, inner), fwd3),
                   pl.BlockSpec((1, q, inner), bwd3),
                   pl.BlockSpec((1, 2, nstate, inner), lambda bi, c: (bi, 0, 0, 0))],
        out_shape=[jax.ShapeDtypeStruct((b, n, inner), BF16),
                   jax.ShapeDtypeStruct((b, n, inner), BF16),
                   jax.ShapeDtypeStruct((b, 2, nstate, inner), F32)],
        scratch_shapes=[pltpu.VMEM((2, nstate, inner), F32)],
        compiler_params=_params("parallel", "arbitrary"),
    )(xc, xc, dtt, dtt, bcol, acol, d_e, h0)


def _mix_kernel(ona_ref, ofn_ref, yf_ref, yb_ref, z_ref, gates_ref, x_ref, g1_ref, nw_ref,
                w1_ref, w2_ref, w3_ref, wo_ref, lg_ref, lb_ref, o_ref, *, groups, alpha):
    d = o_ref.shape[1]
    gy = (yf_ref[...].astype(F32) + yb_ref[...].astype(F32)) * _silu(z_ref[...].astype(F32))
    gw = gy.shape[1] // groups
    parts = []
    for g in range(groups):
        sl = slice(g * gw, (g + 1) * gw)
        v = gy[:, sl]
        ms = jnp.mean(v * v, axis=-1, keepdims=True)
        parts.append((v * lax.rsqrt(ms + RMS_EPS) * nw_ref[:, sl]).astype(BF16))
    o_ssm = jnp.concatenate(parts, axis=1)
    gate = lambda k: _sigmoid(gates_ref[:, k * d:(k + 1) * d].astype(F32))
    m = (gate(0) * _dot(ona_ref[...], w1_ref[...])
         + gate(1) * _dot(ofn_ref[...], w2_ref[...])
         + gate(2) * _dot(o_ssm, w3_ref[...]))
    mix = _dot(m.astype(BF16), wo_ref[...])
    y = alpha * x_ref[...] + g1_ref[0] * mix
    o_ref[...] = _ln(y) * lg_ref[...] + lb_ref[...]


def _mix(x2d, o_na, o_fn, y_f, y_b, p2, gate1, rows_per_mod, norm_w, w1, w2, w3, wo, ln_g, ln_b,
         *, off_z, off_g, inner, alpha):
    m, d = x2d.shape
    na = o_na.shape[1]
    fn = o_fn.shape[1]
    tm = _tile(m, 256, 128)
    assert off_z % inner == 0 and off_g % (3 * d) == 0 and rows_per_mod % tm == 0
    row = lambda w, blk=0: pl.BlockSpec((tm, w), lambda i: (i, blk))
    full = lambda r, c: pl.BlockSpec((r, c), lambda i: (0, 0))
    return pl.pallas_call(
        functools.partial(_mix_kernel, groups=SSM_GROUPS, alpha=alpha),
        grid=(m // tm,),
        in_specs=[row(na), row(fn), row(inner), row(inner),
                  row(inner, off_z // inner), row(3 * d, off_g // (3 * d)), row(d),
                  pl.BlockSpec((1, 1, d), lambda i: (i * tm // rows_per_mod, 0, 0)),
                  full(1, inner), full(na, d), full(fn, d), full(inner, d), full(d, d),
                  full(1, d), full(1, d)],
        out_specs=pl.BlockSpec((tm, d), lambda i: (i, 0)),
        out_shape=jax.ShapeDtypeStruct((m, d), F32),
        compiler_params=_params("parallel"),
    )(o_na, o_fn, y_f, y_b, p2, p2, x2d, gate1, norm_w, w1, w2, w3, wo,
      ln_g.reshape(1, d), ln_b.reshape(1, d))


def _ffn_kernel(x_ref, sh_ref, sc_ref, g_ref, wg_ref, wu_ref, wd_ref, lg_ref, lb_ref, o_ref,
                h_scr, acc_scr, *, alpha):
    j = pl.program_id(1)

    @pl.when(j == 0)
    def _():
        h_scr[...] = (_ln(x_ref[...]) * (1.0 + sc_ref[0]) + sh_ref[0]).astype(BF16)
        acc_scr[...] = jnp.zeros_like(acc_scr)

    h = h_scr[...]
    a = (_silu(_dot(h, wg_ref[...])) * _dot(h, wu_ref[...])).astype(BF16)
    acc_scr[...] += _dot(a, wd_ref[...])

    @pl.when(j == pl.num_programs(1) - 1)
    def _():
        y = alpha * x_ref[...] + g_ref[0] * acc_scr[...]
        o_ref[...] = _ln(y) * lg_ref[...] + lb_ref[...]


def _ffn(x2d, shift, scale, gate, rows_per_mod, wg, wu, wd, ln_g, ln_b, alpha):
    m, d = x2d.shape
    hid = wg.shape[1]
    tm = _tile(m, 512, 256, 128)
    th = _tile(hid, 512, 256, 128)
    assert rows_per_mod % tm == 0
    mod = pl.BlockSpec((1, 1, d), lambda i, j: (i * tm // rows_per_mod, 0, 0))
    vec = pl.BlockSpec((1, d), lambda i, j: (0, 0))
    return pl.pallas_call(
        functools.partial(_ffn_kernel, alpha=alpha),
        grid=(m // tm, hid // th),
        in_specs=[pl.BlockSpec((tm, d), lambda i, j: (i, 0)),
                  mod, mod, mod,
                  pl.BlockSpec((d, th), lambda i, j: (0, j)),
                  pl.BlockSpec((d, th), lambda i, j: (0, j)),
                  pl.BlockSpec((th, d), lambda i, j: (j, 0)),
                  vec, vec],
        out_specs=pl.BlockSpec((tm, d), lambda i, j: (i, 0)),
        out_shape=jax.ShapeDtypeStruct((m, d), F32),
        scratch_shapes=[pltpu.VMEM((tm, d), BF16), pltpu.VMEM((tm, d), F32)],
        compiler_params=_params("parallel", "arbitrary"),
    )(x2d, shift, scale, gate, wg, wu, wd, ln_g.reshape(1, d), ln_b.reshape(1, d))


def _rope_tables(n_tokens):
    t = np.arange(n_tokens)
    half = NA_HEAD_DIM // 4
    inv = ROPE_THETA ** (-np.arange(half, dtype=np.float32) / half)
    pos = np.stack([t // GRID_W, t % GRID_W], axis=1).astype(np.float32)
    lane = np.arange(NA_HEAD_DIM)
    ang = pos[:, lane // (2 * half)] * inv[lane % half][None, :]
    sign = np.where((lane % (2 * half)) < half, -1.0, 1.0).astype(np.float32)
    return jnp.asarray(np.cos(ang), F32), jnp.asarray(np.sin(ang) * sign, F32)


def _bias_table(rpb, cases):
    kc = NA_WIN_COLS
    col = np.arange(GRID_W)
    c_start = np.clip(col - kc // 2, 0, GRID_W - kc)
    col_in = (col[None, :] >= c_start[:, None]) & (col[None, :] < c_start[:, None] + kc)
    dc = np.clip(col[None, :] - col[:, None], -(kc - 1), kc - 1) + (kc - 1)
    heads = rpb.shape[0]
    t = jnp.where(col_in[None, None], rpb[:, :, dc].astype(F32) * LOG2E, MASK_VALUE)
    masked = jnp.full((heads, GRID_W, GRID_W), MASK_VALUE, F32)
    tables = []
    for rel_rs, off in cases:
        qrows = []
        for qr in range(NA_QROWS):
            krows = []
            for kr in range(NA_KROWS):
                inside = rel_rs[qr] <= kr < rel_rs[qr] + NA_WIN_ROWS
                krows.append(t[:, kr + off - qr + NA_WIN_ROWS - 1] if inside else masked)
            qrows.append(jnp.stack(krows, axis=2))
        tables.append(jnp.stack(qrows, axis=1).reshape(heads, NA_QROWS * GRID_W, NA_KROWS * GRID_W))
    return jnp.stack(tables, axis=1)


def kernel(x, c, ctx, c_ctx, w_mod, w_in, na_rpb, ssm_conv_w, ssm_conv_b, ssm_a_log, ssm_dt_bias, ssm_d, ssm_norm_w, w_br_na, w_br_fn, w_br_ssm, w_out, ln1_g, ln1_b, w_ffn_gate, w_ffn_up, w_ffn_down, ln2_g, ln2_b):
    b, s, d = x.shape
    n_ctx = ctx.shape[1]
    depth = w_mod.shape[0]
    na = w_br_na.shape[1]
    fn = w_br_fn.shape[1]
    inner = w_br_ssm.shape[1]
    conv_dim = ssm_conv_w.shape[2]
    ssm_heads = ssm_d.shape[1]
    na_heads = na_rpb.shape[1]
    assert na == na_heads * NA_HEAD_DIM and inner == ssm_heads * SSM_HEAD_DIM and s % GRID_W == 0
    alpha = (2.0 * depth) ** 0.25

    c_u = 3 * na + fn
    c_z = c_u + conv_dim
    c_dt = c_z + inner
    c_g = c_dt + 2 * ssm_heads
    off_q, off_k, off_v, off_u = 0, na, 2 * na, 3 * na
    off_z = c_u
    off_g = off_z + inner
    off_xbc = off_g + 3 * d
    assert off_u % fn == 0

    rows = b + 1
    rows_pad = -(-rows // MOD_ROWS_ALIGN) * MOD_ROWS_ALIGN
    c_rows = jnp.concatenate([c, c_ctx[None, :], jnp.zeros((rows_pad - rows, d), F32)], axis=0)
    mods = _mods(c_rows, w_mod)

    cos_t, sin_t = _rope_tables(s)
    na_blocks, na_cases = _na_plan(s // GRID_W)
    cs_x, dm_x = _dft_tables(s, fn // FN_GROUPS)
    cs_c, dm_c = _dft_tables(n_ctx, fn // FN_GROUPS)
    nstate = (conv_dim - inner) // (2 * SSM_GROUPS)
    zero_state = jnp.zeros((b, 2, nstate, inner), F32)

    x2 = x.reshape(b * s, d)
    c2 = ctx.reshape(b * n_ctx, d)
    flat = lambda t: t.reshape(-1, t.shape[-1])
    for l in range(depth):
        with_ctx = l < depth - 1
        wl = w_in[l]
        w_main = jnp.concatenate([wl[:, :c_u], wl[:, c_z:c_dt], wl[:, c_g:], wl[:, c_u:c_z]], axis=1).astype(BF16)
        w_dtt = wl[:, c_dt:c_g].astype(BF16).T
        mx = [v.reshape(b, 1, d) for v in jnp.split(mods[l, :b], 6, axis=-1)]
        mc = [v.reshape(1, 1, d) for v in jnp.split(mods[l, b:b + 1], 6, axis=-1)]
        a_neg = -jnp.exp(ssm_a_log[l].astype(F32))
        d_e = jnp.repeat(ssm_d[l].astype(F32), SSM_HEAD_DIM).reshape(1, inner)
        norm_w = ssm_norm_w[l].astype(F32).reshape(1, inner)
        w1, w2, w3 = w_br_na[l].astype(BF16), w_br_fn[l].astype(BF16), w_br_ssm[l].astype(BF16)
        wo = w_out[l].astype(BF16)
        wg, wu, wd = w_ffn_gate[l].astype(BF16), w_ffn_up[l].astype(BF16), w_ffn_down[l].astype(BF16)
        bias_t = _bias_table(na_rpb[l], na_cases)
        mix_kw = dict(off_z=off_z, off_g=off_g, inner=inner, alpha=alpha)

        px, dttx = _inproj(x2, mx[0], mx[1], s, w_main, w_dtt)
        pc, dttc = _inproj(c2, mc[0], mc[1], b * n_ctx, w_main, w_dtt)
        px3 = px.reshape(b, s, -1)
        pc3 = pc.reshape(b, n_ctx, -1)

        o_na = _na_latent(px3, pc3, cos_t, sin_t, bias_t, na_blocks, heads=na_heads,
                          off_q=off_q, off_k=off_k, off_v=off_v)
        o_fn = _fourier(px3, cs_x, dm_x, off_u=off_u, width=fn)
        xc_c = _conv(pc3, ssm_conv_w[l], ssm_conv_b[l], off_xbc=off_xbc)
        yf_c, yb_c, h_ctx = _ssd(xc_c, dttc, ssm_dt_bias[l], a_neg, d_e, zero_state, heads=ssm_heads, inner=inner)
        xc_x = _conv(px3, ssm_conv_w[l], ssm_conv_b[l], off_xbc=off_xbc)
        yf_x, yb_x, _ = _ssd(xc_x, dttx, ssm_dt_bias[l], a_neg, d_e, h_ctx, heads=ssm_heads, inner=inner)

        x2 = _mix(x2, flat(o_na), flat(o_fn), flat(yf_x), flat(yb_x), px, mx[2], s, norm_w,
                  w1, w2, w3, wo, ln1_g[l], ln1_b[l], **mix_kw)
        x2 = _ffn(x2, mx[3], mx[4], mx[5], s, wg, wu, wd, ln2_g[l], ln2_b[l], alpha)
        if with_ctx:
            o_na_c = _ctx_attn(pc3, heads=na_heads, off_q=off_q, off_k=off_k, off_v=off_v)
            o_fn_c = _fourier(pc3, cs_c, dm_c, off_u=off_u, width=fn)
            c2 = _mix(c2, flat(o_na_c), flat(o_fn_c), flat(yf_c), flat(yb_c), pc, mc[2], b * n_ctx, norm_w,
                      w1, w2, w3, wo, ln1_g[l], ln1_b[l], **mix_kw)
            c2 = _ffn(c2, mc[3], mc[4], mc[5], b * n_ctx, wg, wu, wd, ln2_g[l], ln2_b[l], alpha)
    return x2.reshape(b, s, d)
```

```python
import functools
import math

import jax
import jax.numpy as jnp
import numpy as np
from jax import lax
from jax.experimental import pallas as pl
from jax.experimental.pallas import tpu as pltpu

F32 = jnp.float32
BF16 = jnp.bfloat16

GRID_W = 64
NA_WIN_ROWS = 8
NA_WIN_COLS = 16
NA_HEAD_DIM = 128
ROPE_THETA = 10000.0
FN_GROUPS = 4
SSM_GROUPS = 4
SSM_HEAD_DIM = 64
SSM_CHUNK = 128
LN_EPS = 1e-6
RMS_EPS = 1e-5
MASK_VALUE = -1e30
LOG2E = 1.0 / math.log(2.0)

NA_QROWS = 4
NA_KROWS = NA_WIN_ROWS + NA_QROWS

V7X_VMEM_BYTES = 64 * 1024 * 1024
VMEM_LIMIT_BYTES = V7X_VMEM_BYTES - 8 * 1024 * 1024
V7X_LANES = 128
V7X_SUBLANES = 8
MOD_ROWS_ALIGN = 16
ROW_SLAB = 256


def _params(*sem):
    return pltpu.CompilerParams(dimension_semantics=sem, vmem_limit_bytes=VMEM_LIMIT_BYTES)


def _tile(n, *candidates):
    for c in candidates:
        if n % c == 0:
            return c
    return n


def _sigmoid(x):
    return 1.0 / (1.0 + jnp.exp(-x))


def _silu(x):
    return x * _sigmoid(x)


def _softplus(x):
    return jnp.maximum(x, 0.0) + jnp.log(1.0 + jnp.exp(-jnp.abs(x)))


def _ln(x):
    mu = jnp.mean(x, axis=-1, keepdims=True)
    xc = x - mu
    var = jnp.mean(xc * xc, axis=-1, keepdims=True)
    return xc * lax.rsqrt(var + LN_EPS)


def _dot(a, b):
    return jnp.dot(a, b, preferred_element_type=F32)


def _dot_nt(a, b):
    return lax.dot_general(a, b, (((1,), (1,)), ((), ())), preferred_element_type=F32)


def _split3(a):
    hi = a.astype(BF16)
    r = a - hi.astype(F32)
    mid = r.astype(BF16)
    lo = (r - mid.astype(F32)).astype(BF16)
    return hi, mid, lo


def _dot_sel_l(a, sel):
    hi, mid, lo = _split3(a)
    return _dot(hi, sel) + _dot(mid, sel) + _dot(lo, sel)


def _dot_sel_nt(sel, a):
    hi, mid, lo = _split3(a)
    return _dot_nt(sel, hi) + _dot_nt(sel, mid) + _dot_nt(sel, lo)


def _mods_kernel(c_ref, w_ref, o_ref):
    a = _silu(c_ref[...]).astype(BF16)
    o_ref[0] = _dot(a, w_ref[0].astype(BF16))


def _mods(c_rows, w_mod):
    depth, d, n = w_mod.shape
    r = c_rows.shape[0]
    tn = _tile(n, 1024, 512, 256, 128)
    return pl.pallas_call(
        _mods_kernel,
        grid=(depth, n // tn),
        in_specs=[pl.BlockSpec((r, d), lambda l, j: (0, 0)),
                  pl.BlockSpec((1, d, tn), lambda l, j: (l, 0, j))],
        out_specs=pl.BlockSpec((1, r, tn), lambda l, j: (l, 0, j)),
        out_shape=jax.ShapeDtypeStruct((depth, r, n), F32),
        compiler_params=_params("parallel", "parallel"),
    )(c_rows, w_mod)


def _slabs(tm):
    step = _tile(tm, ROW_SLAB)
    return [slice(r, r + step) for r in range(0, tm, step)]


def _inproj_kernel(x_ref, sh_ref, sc_ref, w_ref, wdtt_ref, o_ref, dtt_ref, h_scr):
    j = pl.program_id(1)

    @pl.when(j == 0)
    def _():
        for rows in _slabs(x_ref.shape[0]):
            h = (_ln(x_ref[rows, :]) * (1.0 + sc_ref[0, 0]) + sh_ref[0, 0]).astype(BF16)
            h_scr[rows, :] = h
            dtt_ref[:, rows] = _dot_nt(wdtt_ref[...], h)
            o_ref[rows, :] = _dot(h, w_ref[...]).astype(BF16)

    @pl.when(j > 0)
    def _():
        o_ref[...] = _dot(h_scr[...], w_ref[...]).astype(BF16)


class _Mods:
    def __init__(self, table, layer, row0, rows_per_mod):
        self.table, self.layer, self.row0, self.rows_per_mod = table, layer, row0, rows_per_mod
        self.d = table.shape[-1] // 6

    def spec(self, chunk, tm):
        assert self.rows_per_mod % tm == 0
        layer, row0, rpm = self.layer, self.row0, self.rows_per_mod
        return pl.BlockSpec((1, 1, 1, self.d), lambda i, *_: (layer, row0 + i * tm // rpm, 0, chunk))


def _inproj(x2d, mods, w_main, w_dtt):
    m, d = x2d.shape
    n = w_main.shape[1]
    ndt = w_dtt.shape[0]
    tm = _tile(m, 1024, 512, 256, 128)
    tn = _tile(n, 1536, 1024, 512, 256, 128)
    return pl.pallas_call(
        _inproj_kernel,
        grid=(m // tm, n // tn),
        in_specs=[pl.BlockSpec((tm, d), lambda i, j: (i, 0)),
                  mods.spec(0, tm), mods.spec(1, tm),
                  pl.BlockSpec((d, tn), lambda i, j: (0, j)),
                  pl.BlockSpec((ndt, d), lambda i, j: (0, 0))],
        out_specs=[pl.BlockSpec((tm, tn), lambda i, j: (i, j)),
                   pl.BlockSpec((ndt, tm), lambda i, j: (0, i))],
        out_shape=[jax.ShapeDtypeStruct((m, n), BF16),
                   jax.ShapeDtypeStruct((ndt, m), F32)],
        scratch_shapes=[pltpu.VMEM((tm, d), BF16)],
        compiler_params=_params("parallel", "arbitrary"),
    )(x2d, mods.table, mods.table, w_main, w_dtt)


def _na_plan(rows):
    assert rows % NA_QROWS == 0 and rows >= NA_KROWS
    blocks, cases = [], []
    for r0 in range(0, rows, NA_QROWS):
        start = min(max(r0 - NA_WIN_ROWS // 2, 0), rows - NA_KROWS)
        rs = [min(max(r - NA_WIN_ROWS // 2, 0), rows - NA_WIN_ROWS) for r in range(r0, r0 + NA_QROWS)]
        case = (tuple(v - start for v in rs), start - r0)
        if case not in cases:
            cases.append(case)
        blocks.append((r0, start, cases.index(case)))
    return blocks, cases


def _na_kernel(q_ref, k_ref, v_ref, kc_ref, vc_ref, cos_ref, sin_ref, bias_ref, o_ref,
               qr_scr, qs_scr, kr_scr, *, blocks, scale):
    s, hd = qr_scr.shape
    half = hd // 4
    src = lax.broadcasted_iota(jnp.int32, (hd, hd), 0)
    dst = lax.broadcasted_iota(jnp.int32, (hd, hd), 1)
    partner = jnp.where((dst % (2 * half)) < half, dst + half, dst - half)
    perm = jnp.where(src == partner, 1.0, 0.0).astype(BF16)

    def rope(t):
        return t.astype(F32) * cos_ref[...] + _dot(t, perm) * sin_ref[...]

    qscale = scale * LOG2E
    qr_scr[...] = (rope(q_ref[0]) * qscale).astype(BF16)
    qs_scr[...] = (q_ref[0].astype(F32) * qscale).astype(BF16)
    kr_scr[...] = rope(k_ref[0]).astype(BF16)
    kc = kc_ref[0]
    vc = vc_ref[0]
    nq = NA_QROWS * GRID_W
    nk = NA_KROWS * GRID_W
    for r0, start, case in blocks:
        qs = slice(r0 * GRID_W, r0 * GRID_W + nq)
        ks = slice(start * GRID_W, start * GRID_W + nk)
        s_lat = _dot_nt(qr_scr[qs, :], kr_scr[ks, :]) + bias_ref[0, case]
        s_ctx = _dot_nt(qs_scr[qs, :], kc)
        mx = jnp.maximum(jnp.max(s_lat, axis=-1, keepdims=True), jnp.max(s_ctx, axis=-1, keepdims=True))
        p_lat = jnp.exp2(s_lat - mx)
        p_ctx = jnp.exp2(s_ctx - mx)
        den = jnp.sum(p_lat, axis=-1, keepdims=True) + jnp.sum(p_ctx, axis=-1, keepdims=True)
        p = jnp.concatenate([p_lat.astype(BF16), p_ctx.astype(BF16)], axis=1)
        vals = jnp.concatenate([v_ref[0, ks, :], vc], axis=0)
        o_ref[0, qs, :] = (_dot(p, vals) * (1.0 / den)).astype(BF16)


def _na_latent(px, pc, cos_t, sin_t, bias_t, blocks, *, heads, off_q, off_k, off_v):
    b, s, _ = px.shape
    n_ctx = pc.shape[1]
    hd = NA_HEAD_DIM
    col = lambda off: (lambda h, bi: (bi, 0, off // hd + h))
    kern = functools.partial(_na_kernel, blocks=blocks, scale=hd ** -0.5)
    return pl.pallas_call(
        kern,
        grid=(heads, b),
        in_specs=[pl.BlockSpec((1, s, hd), col(off_q)),
                  pl.BlockSpec((1, s, hd), col(off_k)),
                  pl.BlockSpec((1, s, hd), col(off_v)),
                  pl.BlockSpec((1, n_ctx, hd), col(off_k)),
                  pl.BlockSpec((1, n_ctx, hd), col(off_v)),
                  pl.BlockSpec((s, hd), lambda h, bi: (0, 0)),
                  pl.BlockSpec((s, hd), lambda h, bi: (0, 0)),
                  pl.BlockSpec((1,) + bias_t.shape[1:], lambda h, bi: (h, 0, 0, 0))],
        out_specs=pl.BlockSpec((1, s, hd), lambda h, bi: (bi, 0, h)),
        out_shape=jax.ShapeDtypeStruct((b, s, heads * hd), BF16),
        scratch_shapes=[pltpu.VMEM((s, hd), BF16)] * 3,
        compiler_params=_params("parallel", "parallel"),
    )(px, px, px, pc, pc, cos_t, sin_t, bias_t)


def _ctx_attn_kernel(q_ref, k_ref, v_ref, o_ref, *, scale):
    s = _dot_nt(q_ref[0], k_ref[0]) * scale
    p = jnp.exp(s - jnp.max(s, axis=-1, keepdims=True))
    den = jnp.sum(p, axis=-1, keepdims=True)
    o_ref[0] = (_dot(p.astype(BF16), v_ref[0]) / den).astype(BF16)


def _ctx_attn(pc, *, heads, off_q, off_k, off_v):
    b, n, _ = pc.shape
    hd = NA_HEAD_DIM
    col = lambda off: (lambda bi, h: (bi, 0, off // hd + h))
    return pl.pallas_call(
        functools.partial(_ctx_attn_kernel, scale=hd ** -0.5),
        grid=(b, heads),
        in_specs=[pl.BlockSpec((1, n, hd), col(off_q)),
                  pl.BlockSpec((1, n, hd), col(off_k)),
                  pl.BlockSpec((1, n, hd), col(off_v))],
        out_specs=pl.BlockSpec((1, n, hd), lambda bi, h: (bi, 0, h)),
        out_shape=jax.ShapeDtypeStruct((b, n, heads * hd), BF16),
        compiler_params=_params("parallel", "parallel"),
    )(pc, pc, pc)


def _fn1_kernel(u_ref, cs_ref, yc_ref, ys_ref, *, groups, gd):
    for g in range(groups):
        sl = slice(g * gd, (g + 1) * gd)
        y = _dot(u_ref[0, :, sl], cs_ref[...])
        yc_ref[0, :, sl] = y[:, :gd].astype(BF16)
        ys_ref[0, :, sl] = y[:, gd:].astype(BF16)


def _fn2_kernel(dc_ref, ds_ref, yc_ref, ys_ref, o_ref):
    o_ref[0] = (_dot(dc_ref[...], yc_ref[0]) + _dot(ds_ref[...], ys_ref[0])).astype(BF16)


def _cos_sin_outer(n):
    r = _tile(n, 64, 32, 16)
    s = jnp.arange(n, dtype=jnp.int32)[None, :]
    unit = 2.0 * math.pi / n
    ang_hi = ((jnp.arange(n // r, dtype=jnp.int32)[:, None] * r * s) % n).astype(F32) * unit
    ang_lo = ((jnp.arange(r, dtype=jnp.int32)[:, None] * s) % n).astype(F32) * unit
    ch, sh = jnp.cos(ang_hi)[:, None, :], jnp.sin(ang_hi)[:, None, :]
    cl, sl = jnp.cos(ang_lo)[None, :, :], jnp.sin(ang_lo)[None, :, :]
    return (ch * cl - sh * sl).reshape(n, n), (sh * cl + ch * sl).reshape(n, n)


def _dft_tables(s, gd):
    cc, sc = _cos_sin_outer(gd)
    cs = jnp.concatenate([cc, sc], axis=1) * (1.0 / math.sqrt(s * gd))
    cp, sp = _cos_sin_outer(s)
    return cs.astype(BF16), cp.astype(BF16), (-sp).astype(BF16)


def _fourier(p3, cs, dc, ds, *, off_u, width):
    b, s, _ = p3.shape
    gd = width // FN_GROUPS
    ts = _tile(s, 1024, 512, 256, 128)
    stage1 = jax.ShapeDtypeStruct((b, s, width), BF16)
    yc, ys = pl.pallas_call(
        functools.partial(_fn1_kernel, groups=FN_GROUPS, gd=gd),
        grid=(b, s // ts),
        in_specs=[pl.BlockSpec((1, ts, width), lambda bi, i: (bi, i, off_u // width)),
                  pl.BlockSpec((gd, 2 * gd), lambda bi, i: (0, 0))],
        out_specs=[pl.BlockSpec((1, ts, width), lambda bi, i: (bi, i, 0))] * 2,
        out_shape=[stage1, stage1],
        compiler_params=_params("parallel", "parallel"),
    )(p3, cs)
    tm = _tile(s, 1024, 512, 256, 128)
    tn = _tile(width, 512, 256, 128)
    dspec = pl.BlockSpec((tm, s), lambda i, bi, j: (i, 0))
    yspec = pl.BlockSpec((1, s, tn), lambda i, bi, j: (bi, 0, j))
    return pl.pallas_call(
        _fn2_kernel,
        grid=(s // tm, b, width // tn),
        in_specs=[dspec, dspec, yspec, yspec],
        out_specs=pl.BlockSpec((1, tm, tn), lambda i, bi, j: (bi, i, j)),
        out_shape=jax.ShapeDtypeStruct((b, s, width), BF16),
        compiler_params=_params("parallel", "parallel", "parallel"),
    )(dc, ds, yc, ys)


def _conv_kernel(x_ref, w_ref, b_ref, o_ref):
    x = x_ref[0].astype(F32)
    n = x.shape[0]
    taps = w_ref.shape[0]
    pad = taps // 2
    halo = jnp.zeros((V7X_SUBLANES, x.shape[1]), F32)
    xp = jnp.concatenate([halo, x, halo], axis=0)
    acc = b_ref[...] + w_ref[pad:pad + 1, :] * x
    for k in range(taps):
        sh = pad - k
        if sh == 0:
            continue
        shifted = pltpu.roll(xp, sh % xp.shape[0], 0)[V7X_SUBLANES:V7X_SUBLANES + n]
        acc = acc + w_ref[k:k + 1, :] * shifted
    o_ref[0] = _silu(acc).astype(BF16)


def _conv(p3, conv_w, conv_b, *, off_xbc):
    b, n, _ = p3.shape
    taps, cd = conv_w.shape
    tc = _tile(cd, 512, 256, 128)
    assert off_xbc % tc == 0
    return pl.pallas_call(
        _conv_kernel,
        grid=(b, cd // tc),
        in_specs=[pl.BlockSpec((1, n, tc), lambda bi, j: (bi, 0, off_xbc // tc + j)),
                  pl.BlockSpec((taps, tc), lambda bi, j: (0, j)),
                  pl.BlockSpec((1, tc), lambda bi, j: (0, j))],
        out_specs=pl.BlockSpec((1, n, tc), lambda bi, j: (bi, 0, j)),
        out_shape=jax.ShapeDtypeStruct((b, n, cd), BF16),
        compiler_params=_params("parallel", "parallel"),
    )(p3, conv_w, conv_b.reshape(1, cd))


def _ssd_kernel(xf_ref, xb_ref, dttf_ref, dttb_ref, bcol_ref, acol_ref, dsk_ref, h0_ref,
                yf_ref, yb_ref, hout_ref, st_scr, *, heads, inner, groups, nstate):
    c = pl.program_id(1)
    nc = pl.num_programs(1)
    q = xf_ref.shape[1]
    hp = inner // heads
    hpg = heads // groups
    gw = inner // groups
    lanes = V7X_LANES
    hpt = lanes // hp
    tpg = hpg // hpt

    @pl.when(c == 0)
    def _():
        st_scr[...] = h0_ref[0]

    li = lax.broadcasted_iota(jnp.int32, (q, q), 0)
    si = lax.broadcasted_iota(jnp.int32, (q, q), 1)
    lower = li >= si
    upper = li <= si
    lane_head = lax.broadcasted_iota(jnp.int32, (q, lanes), 1) // hp

    for d, (x_ref, dtt_ref, y_ref) in enumerate(((xf_ref, dttf_ref, yf_ref), (xb_ref, dttb_ref, yb_ref))):
        hs = slice(d * heads, (d + 1) * heads)
        mask = lower if d == 0 else upper
        tri = jnp.where(mask, 1.0, 0.0).astype(BF16)
        dtt = _softplus(dtt_ref[hs, :] + bcol_ref[hs, :])
        a_t = dtt * (acol_ref[hs, :] * LOG2E)
        acum = _dot_sel_nt(tri, a_t)
        acum_t = _dot_sel_l(a_t, jnp.where(upper if d == 0 else lower, 1.0, 0.0).astype(BF16))
        total = acum[q - 1:q, :] if d == 0 else acum[0:1, :]
        total_t = acum_t[:, q - 1:q] if d == 0 else acum_t[:, 0:1]
        w_t = dtt * jnp.exp2(total_t - acum_t)
        row_t = acum_t - jnp.log2(dtt)
        tot = jnp.exp2(total)
        for g in range(groups):
            bg = x_ref[0, :, inner + g * nstate:inner + (g + 1) * nstate]
            cg = x_ref[0, :, inner + (groups + g) * nstate:inner + (groups + g + 1) * nstate]
            cb = _dot_nt(cg, bg)
            bg_t = bg.astype(F32).T
            y_off = _dot(cg, st_scr[d, :, g * gw:(g + 1) * gw].astype(BF16))
            for t in range(tpg):
                h0 = g * hpg + t * hpt
                ls = slice(h0 * hp, h0 * hp + lanes)
                xt = x_ref[0, :, ls]
                lhs_y, lhs_s, rhs_x = [], [], []
                esc = tot_l = None
                for j in range(hpt):
                    h = h0 + j
                    col = acum[:, h:h + 1]
                    lmat = jnp.exp2(jnp.where(mask, col - row_t[h:h + 1, :], MASK_VALUE))
                    lhs_y.append((cb * lmat).astype(BF16))
                    lhs_s.append((bg_t * w_t[h:h + 1, :]).astype(BF16))
                    own = lane_head == j
                    rhs_x.append(jnp.where(own, xt, jnp.zeros_like(xt)))
                    esc_j = jnp.exp2(jnp.broadcast_to(col, (q, lanes)))
                    tot_j = jnp.broadcast_to(tot[:, h:h + 1], (1, lanes))
                    esc = esc_j if j == 0 else jnp.where(own, esc_j, esc)
                    tot_l = tot_j if j == 0 else jnp.where(own[0:1], tot_j, tot_l)
                x_bd = jnp.concatenate(rhs_x, axis=0)
                y = _dot(jnp.concatenate(lhs_y, axis=1), x_bd) + esc * y_off[:, t * lanes:(t + 1) * lanes]
                if d == 0:
                    y = y + dsk_ref[:, ls] * xt.astype(F32)
                y_ref[0, :, ls] = y.astype(BF16)
                st_scr[d, :, ls] = st_scr[d, :, ls] * tot_l + _dot(jnp.concatenate(lhs_s, axis=1), x_bd)

    @pl.when(c == nc - 1)
    def _():
        hout_ref[0] = st_scr[...]


def _ssd(xc, dtt, dt_bias, a_neg, d_e, h0, *, heads, inner):
    b, n, cd = xc.shape
    q = SSM_CHUNK
    assert n % q == 0
    nc = n // q
    groups = SSM_GROUPS
    nstate = (cd - inner) // (2 * groups)
    hp = inner // heads
    assert V7X_LANES % hp == 0 and (heads // groups) % (V7X_LANES // hp) == 0 and nstate <= q
    nh2 = 2 * heads
    bcol = dt_bias.reshape(nh2, 1)
    acol = a_neg.reshape(nh2, 1)
    fwd3 = lambda bi, c: (bi, c, 0)
    bwd3 = lambda bi, c: (bi, nc - 1 - c, 0)
    const2 = lambda bi, c: (0, 0)
    kern = functools.partial(_ssd_kernel, heads=heads, inner=inner, groups=groups, nstate=nstate)
    return pl.pallas_call(
        kern,
        grid=(b, nc),
        in_specs=[pl.BlockSpec((1, q, cd), fwd3),
                  pl.BlockSpec((1, q, cd), bwd3),
                  pl.BlockSpec((nh2, q), lambda bi, c: (0, bi * nc + c)),
                  pl.BlockSpec((nh2, q), lambda bi, c: (0, bi * nc + nc - 1 - c)),
                  pl.BlockSpec((nh2, 1), const2),
                  pl.BlockSpec((nh2, 1), const2),
                  pl.BlockSpec((1, inner), const2),
                  pl.BlockSpec((1, 2, nstate, inner), lambda bi, c: (bi, 0, 0, 0))],
        out_specs=[pl.BlockSpec((1, q, inner), fwd3),
                   pl.BlockSpec((1, q, inner), bwd3),
                   pl.BlockSpec((1, 2, nstate, inner), lambda bi, c: (bi, 0, 0, 0))],
        out_shape=[jax.ShapeDtypeStruct((b, n, inner), BF16),
                   jax.ShapeDtypeStruct((b, n, inner), BF16),
                   jax.ShapeDtypeStruct((b, 2, nstate, inner), F32)],
        scratch_shapes=[pltpu.VMEM((2, nstate, inner), F32)],
        compiler_params=_params("parallel", "arbitrary"),
    )(xc, xc, dtt, dtt, bcol, acol, d_e, h0)


def _mix_kernel(ona_ref, ofn_ref, yf_ref, yb_ref, z_ref, gates_ref, x_ref, g1_ref, nw_ref,
                w1_ref, w2_ref, w3_ref, wo_ref, lg_ref, lb_ref, o_ref, *, groups, alpha):
    d = o_ref.shape[1]
    gy = (yf_ref[...].astype(F32) + yb_ref[...].astype(F32)) * _silu(z_ref[...].astype(F32))
    gw = gy.shape[1] // groups
    parts = []
    for g in range(groups):
        sl = slice(g * gw, (g + 1) * gw)
        v = gy[:, sl]
        ms = jnp.mean(v * v, axis=-1, keepdims=True)
        parts.append((v * lax.rsqrt(ms + RMS_EPS) * nw_ref[:, sl]).astype(BF16))
    o_ssm = jnp.concatenate(parts, axis=1)
    gate = lambda k: _sigmoid(gates_ref[:, k * d:(k + 1) * d].astype(F32))
    m = (gate(0) * _dot(ona_ref[...], w1_ref[...])
         + gate(1) * _dot(ofn_ref[...], w2_ref[...])
         + gate(2) * _dot(o_ssm, w3_ref[...]))
    mix = _dot(m.astype(BF16), wo_ref[...])
    y = alpha * x_ref[...] + g1_ref[0, 0] * mix
    o_ref[...] = _ln(y) * lg_ref[...] + lb_ref[...]


def _mix(x2d, o_na, o_fn, y_f, y_b, p2, mods, norm_w, w1, w2, w3, wo, ln_g, ln_b,
         *, off_z, off_g, inner, alpha):
    m, d = x2d.shape
    na = o_na.shape[1]
    fn = o_fn.shape[1]
    tm = _tile(m, 256, 128)
    assert off_z % inner == 0 and off_g % (3 * d) == 0
    row = lambda w, blk=0: pl.BlockSpec((tm, w), lambda i: (i, blk))
    full = lambda r, c: pl.BlockSpec((r, c), lambda i: (0, 0))
    return pl.pallas_call(
        functools.partial(_mix_kernel, groups=SSM_GROUPS, alpha=alpha),
        grid=(m // tm,),
        in_specs=[row(na), row(fn), row(inner), row(inner),
                  row(inner, off_z // inner), row(3 * d, off_g // (3 * d)), row(d),
                  mods.spec(2, tm),
                  full(1, inner), full(na, d), full(fn, d), full(inner, d), full(d, d),
                  full(1, d), full(1, d)],
        out_specs=pl.BlockSpec((tm, d), lambda i: (i, 0)),
        out_shape=jax.ShapeDtypeStruct((m, d), F32),
        compiler_params=_params("parallel"),
    )(o_na, o_fn, y_f, y_b, p2, p2, x2d, mods.table, norm_w, w1, w2, w3, wo,
      ln_g.reshape(1, d), ln_b.reshape(1, d))


def _ffn_kernel(x_ref, sh_ref, sc_ref, g_ref, wg_ref, wu_ref, wd_ref, lg_ref, lb_ref, o_ref,
                h_scr, acc_scr, *, alpha):
    j = pl.program_id(1)
    last = pl.num_programs(1) - 1
    slabs = _slabs(x_ref.shape[0])

    def down(h):
        a = (_silu(_dot(h, wg_ref[...])) * _dot(h, wu_ref[...])).astype(BF16)
        return _dot(a, wd_ref[...])

    @pl.when(j == 0)
    def _():
        for rows in slabs:
            h = (_ln(x_ref[rows, :]) * (1.0 + sc_ref[0, 0]) + sh_ref[0, 0]).astype(BF16)
            h_scr[rows, :] = h
            acc_scr[rows, :] = down(h)

    @pl.when((j > 0) & (j < last))
    def _():
        acc_scr[...] += down(h_scr[...])

    @pl.when(j == last)
    def _():
        for rows in slabs:
            y = alpha * x_ref[rows, :] + g_ref[0, 0] * (acc_scr[rows, :] + down(h_scr[rows, :]))
            o_ref[rows, :] = _ln(y) * lg_ref[...] + lb_ref[...]


def _ffn(x2d, mods, wg, wu, wd, ln_g, ln_b, alpha):
    m, d = x2d.shape
    hid = wg.shape[1]
    tm = _tile(m, 512, 256, 128)
    th = _tile(hid, 512, 256, 128)
    assert hid // th >= 2
    vec = pl.BlockSpec((1, d), lambda i, j: (0, 0))
    return pl.pallas_call(
        functools.partial(_ffn_kernel, alpha=alpha),
        grid=(m // tm, hid // th),
        in_specs=[pl.BlockSpec((tm, d), lambda i, j: (i, 0)),
                  mods.spec(3, tm), mods.spec(4, tm), mods.spec(5, tm),
                  pl.BlockSpec((d, th), lambda i, j: (0, j)),
                  pl.BlockSpec((d, th), lambda i, j: (0, j)),
                  pl.BlockSpec((th, d), lambda i, j: (j, 0)),
                  vec, vec],
        out_specs=pl.BlockSpec((tm, d), lambda i, j: (i, 0)),
        out_shape=jax.ShapeDtypeStruct((m, d), F32),
        scratch_shapes=[pltpu.VMEM((tm, d), BF16), pltpu.VMEM((tm, d), F32)],
        compiler_params=_params("parallel", "arbitrary"),
    )(x2d, mods.table, mods.table, mods.table, wg, wu, wd, ln_g.reshape(1, d), ln_b.reshape(1, d))


def _rope_tables(n_tokens):
    t = np.arange(n_tokens)
    half = NA_HEAD_DIM // 4
    inv = ROPE_THETA ** (-np.arange(half, dtype=np.float32) / half)
    pos = np.stack([t // GRID_W, t % GRID_W], axis=1).astype(np.float32)
    lane = np.arange(NA_HEAD_DIM)
    ang = pos[:, lane // (2 * half)] * inv[lane % half][None, :]
    sign = np.where((lane % (2 * half)) < half, -1.0, 1.0).astype(np.float32)
    return jnp.asarray(np.cos(ang), F32), jnp.asarray(np.sin(ang) * sign, F32)


def _bias_table(rpb, cases):
    kc = NA_WIN_COLS
    col = np.arange(GRID_W)
    c_start = np.clip(col - kc // 2, 0, GRID_W - kc)
    col_in = (col[None, :] >= c_start[:, None]) & (col[None, :] < c_start[:, None] + kc)
    dc = np.clip(col[None, :] - col[:, None], -(kc - 1), kc - 1) + (kc - 1)
    heads = rpb.shape[0]
    t = jnp.where(col_in[None, None], rpb[:, :, dc].astype(F32) * LOG2E, MASK_VALUE)
    masked = jnp.full((heads, GRID_W, GRID_W), MASK_VALUE, F32)
    tables = []
    for rel_rs, off in cases:
        qrows = []
        for qr in range(NA_QROWS):
            krows = []
            for kr in range(NA_KROWS):
                inside = rel_rs[qr] <= kr < rel_rs[qr] + NA_WIN_ROWS
                krows.append(t[:, kr + off - qr + NA_WIN_ROWS - 1] if inside else masked)
            qrows.append(jnp.stack(krows, axis=2))
        tables.append(jnp.stack(qrows, axis=1).reshape(heads, NA_QROWS * GRID_W, NA_KROWS * GRID_W))
    return jnp.stack(tables, axis=1)


def kernel(x, c, ctx, c_ctx, w_mod, w_in, na_rpb, ssm_conv_w, ssm_conv_b, ssm_a_log, ssm_dt_bias, ssm_d, ssm_norm_w, w_br_na, w_br_fn, w_br_ssm, w_out, ln1_g, ln1_b, w_ffn_gate, w_ffn_up, w_ffn_down, ln2_g, ln2_b):
    b, s, d = x.shape
    n_ctx = ctx.shape[1]
    depth = w_mod.shape[0]
    na = w_br_na.shape[1]
    fn = w_br_fn.shape[1]
    inner = w_br_ssm.shape[1]
    conv_dim = ssm_conv_w.shape[2]
    ssm_heads = ssm_d.shape[1]
    na_heads = na_rpb.shape[1]
    assert na == na_heads * NA_HEAD_DIM and inner == ssm_heads * SSM_HEAD_DIM and s % GRID_W == 0
    alpha = (2.0 * depth) ** 0.25

    c_u = 3 * na + fn
    c_z = c_u + conv_dim
    c_dt = c_z + inner
    c_g = c_dt + 2 * ssm_heads
    off_q, off_k, off_v, off_u = 0, na, 2 * na, 3 * na
    off_z = c_u
    off_g = off_z + inner
    off_xbc = off_g + 3 * d
    assert off_u % fn == 0

    rows = b + 1
    rows_pad = -(-rows // MOD_ROWS_ALIGN) * MOD_ROWS_ALIGN
    c_rows = jnp.concatenate([c, c_ctx[None, :], jnp.zeros((rows_pad - rows, d), F32)], axis=0)
    mod_table = _mods(c_rows, w_mod).reshape(depth, rows_pad, 1, 6 * d)

    cos_t, sin_t = _rope_tables(s)
    na_blocks, na_cases = _na_plan(s // GRID_W)
    dft_x = _dft_tables(s, fn // FN_GROUPS)
    dft_c = _dft_tables(n_ctx, fn // FN_GROUPS)
    nstate = (conv_dim - inner) // (2 * SSM_GROUPS)
    zero_state = jnp.zeros((b, 2, nstate, inner), F32)

    x2 = x.reshape(b * s, d)
    c2 = ctx.reshape(b * n_ctx, d)
    flat = lambda t: t.reshape(-1, t.shape[-1])
    for l in range(depth):
        with_ctx = l < depth - 1
        wl = w_in[l]
        w_main = jnp.concatenate([wl[:, :c_u], wl[:, c_z:c_dt], wl[:, c_g:], wl[:, c_u:c_z]], axis=1).astype(BF16)
        w_dtt = wl[:, c_dt:c_g].astype(BF16).T
        mx = _Mods(mod_table, l, 0, s)
        mc = _Mods(mod_table, l, b, b * n_ctx)
        a_neg = -jnp.exp(ssm_a_log[l].astype(F32))
        d_e = jnp.repeat(ssm_d[l].astype(F32), SSM_HEAD_DIM).reshape(1, inner)
        norm_w = ssm_norm_w[l].astype(F32).reshape(1, inner)
        w1, w2, w3 = w_br_na[l].astype(BF16), w_br_fn[l].astype(BF16), w_br_ssm[l].astype(BF16)
        wo = w_out[l].astype(BF16)
        wg, wu, wd = w_ffn_gate[l].astype(BF16), w_ffn_up[l].astype(BF16), w_ffn_down[l].astype(BF16)
        bias_t = _bias_table(na_rpb[l], na_cases)
        mix_kw = dict(off_z=off_z, off_g=off_g, inner=inner, alpha=alpha)

        px, dttx = _inproj(x2, mx, w_main, w_dtt)
        pc, dttc = _inproj(c2, mc, w_main, w_dtt)
        px3 = px.reshape(b, s, -1)
        pc3 = pc.reshape(b, n_ctx, -1)

        o_na = _na_latent(px3, pc3, cos_t, sin_t, bias_t, na_blocks, heads=na_heads,
                          off_q=off_q, off_k=off_k, off_v=off_v)
        o_fn = _fourier(px3, *dft_x, off_u=off_u, width=fn)
        xc_c = _conv(pc3, ssm_conv_w[l], ssm_conv_b[l], off_xbc=off_xbc)
        yf_c, yb_c, h_ctx = _ssd(xc_c, dttc, ssm_dt_bias[l], a_neg, d_e, zero_state, heads=ssm_heads, inner=inner)
        xc_x = _conv(px3, ssm_conv_w[l], ssm_conv_b[l], off_xbc=off_xbc)
        yf_x, yb_x, _ = _ssd(xc_x, dttx, ssm_dt_bias[l], a_neg, d_e, h_ctx, heads=ssm_heads, inner=inner)

        x2 = _mix(x2, flat(o_na), flat(o_fn), flat(yf_x), flat(yb_x), px, mx, norm_w,
                  w1, w2, w3, wo, ln1_g[l], ln1_b[l], **mix_kw)
        x2 = _ffn(x2, mx, wg, wu, wd, ln2_g[l], ln2_b[l], alpha)
        if with_ctx:
            o_na_c = _ctx_attn(pc3, heads=na_heads, off_q=off_q, off_k=off_k, off_v=off_v)
            o_fn_c = _fourier(pc3, *dft_c, off_u=off_u, width=fn)
            c2 = _mix(c2, flat(o_na_c), flat(o_fn_c), flat(yf_c), flat(yb_c), pc, mc, norm_w,
                      w1, w2, w3, wo, ln1_g[l], ln1_b[l], **mix_kw)
            c2 = _ffn(c2, mc, wg, wu, wd, ln2_g[l], ln2_b[l], alpha)
    return x2.reshape(b, s, d)
```

```python
import functools
import math

import jax
import jax.numpy as jnp
import numpy as np
from jax import lax
from jax.experimental import pallas as pl
from jax.experimental.pallas import tpu as pltpu

F32 = jnp.float32
BF16 = jnp.bfloat16

GRID_W = 64
NA_WIN_ROWS = 8
NA_WIN_COLS = 16
NA_HEAD_DIM = 128
ROPE_THETA = 10000.0
FN_GROUPS = 4
SSM_GROUPS = 4
SSM_HEAD_DIM = 64
SSM_CHUNK = 128
LN_EPS = 1e-6
RMS_EPS = 1e-5
MASK_VALUE = -1e30
LOG2E = 1.0 / math.log(2.0)

NA_QROWS = 4
NA_KROWS = NA_WIN_ROWS + NA_QROWS

V7X_VMEM_BYTES = 64 * 1024 * 1024
VMEM_LIMIT_BYTES = V7X_VMEM_BYTES - 8 * 1024 * 1024
V7X_LANES = 128
V7X_SUBLANES = 8
MOD_ROWS_ALIGN = 16
RELAYOUT_TILE = 4 * V7X_LANES
CONV_ROWS = 128
CONV_HALO = 16
ROW_SLAB = 256


def _params(*sem):
    return pltpu.CompilerParams(dimension_semantics=sem, vmem_limit_bytes=VMEM_LIMIT_BYTES)


def _tile(n, *candidates):
    for c in candidates:
        if n % c == 0:
            return c
    return n


def _sigmoid(x):
    return 1.0 / (1.0 + jnp.exp(-x))


def _silu(x):
    return x * _sigmoid(x)


def _softplus(x):
    return jnp.maximum(x, 0.0) + jnp.log(1.0 + jnp.exp(-jnp.abs(x)))


def _ln(x):
    mu = jnp.mean(x, axis=-1, keepdims=True)
    xc = x - mu
    var = jnp.mean(xc * xc, axis=-1, keepdims=True)
    return xc * lax.rsqrt(var + LN_EPS)


def _dot(a, b):
    return jnp.dot(a, b, preferred_element_type=F32)


def _dot_nt(a, b):
    return lax.dot_general(a, b, (((1,), (1,)), ((), ())), preferred_element_type=F32)


def _split3(a):
    hi = a.astype(BF16)
    r = a - hi.astype(F32)
    mid = r.astype(BF16)
    lo = (r - mid.astype(F32)).astype(BF16)
    return hi, mid, lo


def _dot_sel_l(a, sel):
    hi, mid, lo = _split3(a)
    return _dot(hi, sel) + _dot(mid, sel) + _dot(lo, sel)


def _dot_sel_nt(sel, a):
    hi, mid, lo = _split3(a)
    return _dot_nt(sel, hi) + _dot_nt(sel, mid) + _dot_nt(sel, lo)


def _mods_kernel(c_ref, w_ref, o_ref):
    a = _silu(c_ref[...]).astype(BF16)
    o_ref[0] = _dot(a, w_ref[0].astype(BF16))


def _mods(c_rows, w_mod):
    depth, d, n = w_mod.shape
    r = c_rows.shape[0]
    tn = _tile(n, 1024, 512, 256, 128)
    return pl.pallas_call(
        _mods_kernel,
        grid=(depth, n // tn),
        in_specs=[pl.BlockSpec((r, d), lambda l, j: (0, 0)),
                  pl.BlockSpec((1, d, tn), lambda l, j: (l, 0, j))],
        out_specs=pl.BlockSpec((1, r, tn), lambda l, j: (l, 0, j)),
        out_shape=jax.ShapeDtypeStruct((depth, r, n), F32),
        compiler_params=_params("parallel", "parallel"),
    )(c_rows, w_mod)


def _relayout_kernel(at_ref, bt_ref, a_ref, b_ref, o_ref, *, first, stop, shift):
    j = pl.program_id(1)
    shifted = (j >= first) & (j < stop)

    @pl.when(shifted)
    def _():
        keep = a_ref.shape[0] - shift
        o_ref[:keep, :] = a_ref[shift:, :].astype(BF16)
        o_ref[keep:, :] = b_ref[...].astype(BF16)

    @pl.when(jnp.logical_not(shifted))
    def _():
        o_ref[...] = a_ref[...].astype(BF16)


def _relayout_w_in(w_in_t, segments):
    depth, n_src, d = w_in_t.shape
    tw = RELAYOUT_TILE
    n_dst = sum(w for _, w in segments)
    src_tile, shift, first, stop = [], 0, 0, 0
    for src, width in segments:
        assert width % tw == 0
        if src % tw:
            assert shift == 0 and (src % tw) % 16 == 0 and tw % (src % tw) == 0
            shift, first, stop = src % tw, len(src_tile), len(src_tile) + width // tw
        src_tile += [src // tw + t for t in range(width // tw)]
        assert src + width <= n_src
    a_tile = jnp.asarray(src_tile, jnp.int32)
    hb = shift if shift else 16
    b_tile = jnp.asarray([(t + 1) * tw // hb if first <= k < stop else 0 for k, t in enumerate(src_tile)], jnp.int32)
    return pl.pallas_call(
        functools.partial(_relayout_kernel, first=first, stop=stop, shift=shift),
        grid_spec=pltpu.PrefetchScalarGridSpec(
            num_scalar_prefetch=2,
            grid=(depth, n_dst // tw),
            in_specs=[pl.BlockSpec((None, tw, d), lambda l, j, at, bt: (l, at[j], 0)),
                      pl.BlockSpec((None, hb, d), lambda l, j, at, bt: (l, bt[j], 0))],
            out_specs=pl.BlockSpec((None, tw, d), lambda l, j, at, bt: (l, j, 0))),
        out_shape=jax.ShapeDtypeStruct((depth, n_dst, d), BF16),
        compiler_params=_params("parallel", "parallel"),
    )(a_tile, b_tile, w_in_t, w_in_t)


def _slabs(tm):
    step = _tile(tm, ROW_SLAB)
    return [slice(r, r + step) for r in range(0, tm, step)]


def _inproj_kernel(x_ref, sh_ref, sc_ref, w_ref, wdtt_ref, o_ref, dtt_ref, h_scr):
    j = pl.program_id(1)

    @pl.when(j == 0)
    def _():
        for rows in _slabs(x_ref.shape[0]):
            h = (_ln(x_ref[rows, :]) * (1.0 + sc_ref[0, 0]) + sh_ref[0, 0]).astype(BF16)
            h_scr[rows, :] = h
            dtt_ref[:, rows] = _dot_nt(wdtt_ref[...], h)
            o_ref[rows, :] = _dot_nt(h, w_ref[...]).astype(BF16)

    @pl.when(j > 0)
    def _():
        o_ref[...] = _dot_nt(h_scr[...], w_ref[...]).astype(BF16)


class _Mods:
    def __init__(self, table, layer, row0, rows_per_mod):
        self.table, self.layer, self.row0, self.rows_per_mod = table, layer, row0, rows_per_mod
        self.d = table.shape[-1] // 6

    def spec(self, chunk, tm):
        assert self.rows_per_mod % tm == 0
        layer, row0, rpm = self.layer, self.row0, self.rows_per_mod
        return pl.BlockSpec((1, 1, 1, self.d), lambda i, *_: (layer, row0 + i * tm // rpm, 0, chunk))


def _inproj(x2d, mods, w_main, n, w_dtt):
    m, d = x2d.shape
    ndt = w_dtt.shape[0]
    tm = _tile(m, 1024, 512, 256, 128)
    tn = _tile(n, 1536, 1024, 512, 256, 128)
    layer = mods.layer
    return pl.pallas_call(
        _inproj_kernel,
        grid=(m // tm, n // tn),
        in_specs=[pl.BlockSpec((tm, d), lambda i, j: (i, 0)),
                  mods.spec(0, tm), mods.spec(1, tm),
                  pl.BlockSpec((None, tn, d), lambda i, j: (layer, j, 0)),
                  pl.BlockSpec((ndt, d), lambda i, j: (0, 0))],
        out_specs=[pl.BlockSpec((tm, tn), lambda i, j: (i, j)),
                   pl.BlockSpec((ndt, tm), lambda i, j: (0, i))],
        out_shape=[jax.ShapeDtypeStruct((m, n), BF16),
                   jax.ShapeDtypeStruct((ndt, m), F32)],
        scratch_shapes=[pltpu.VMEM((tm, d), BF16)],
        compiler_params=_params("parallel", "arbitrary"),
    )(x2d, mods.table, mods.table, w_main, w_dtt)


def _na_plan(rows):
    assert rows % NA_QROWS == 0 and rows >= NA_KROWS
    blocks, cases = [], []
    for r0 in range(0, rows, NA_QROWS):
        start = min(max(r0 - NA_WIN_ROWS // 2, 0), rows - NA_KROWS)
        rs = [min(max(r - NA_WIN_ROWS // 2, 0), rows - NA_WIN_ROWS) for r in range(r0, r0 + NA_QROWS)]
        case = (tuple(v - start for v in rs), start - r0)
        if case not in cases:
            cases.append(case)
        blocks.append((r0, start, cases.index(case)))
    return blocks, cases


def _na_kernel(q_ref, k_ref, v_ref, kc_ref, vc_ref, cos_ref, sin_ref, bias_ref, o_ref,
               qr_scr, qs_scr, kr_scr, *, blocks, scale):
    s, hd = qr_scr.shape
    half = hd // 4
    src = lax.broadcasted_iota(jnp.int32, (hd, hd), 0)
    dst = lax.broadcasted_iota(jnp.int32, (hd, hd), 1)
    partner = jnp.where((dst % (2 * half)) < half, dst + half, dst - half)
    perm = jnp.where(src == partner, 1.0, 0.0).astype(BF16)

    def rope(t):
        return t.astype(F32) * cos_ref[...] + _dot(t, perm) * sin_ref[...]

    qscale = scale * LOG2E
    qr_scr[...] = (rope(q_ref[0]) * qscale).astype(BF16)
    qs_scr[...] = (q_ref[0].astype(F32) * qscale).astype(BF16)
    kr_scr[...] = rope(k_ref[0]).astype(BF16)
    kc = kc_ref[0]
    vc = vc_ref[0]
    nq = NA_QROWS * GRID_W
    nk = NA_KROWS * GRID_W
    for r0, start, case in blocks:
        qs = slice(r0 * GRID_W, r0 * GRID_W + nq)
        ks = slice(start * GRID_W, start * GRID_W + nk)
        s_lat = _dot_nt(qr_scr[qs, :], kr_scr[ks, :]) + bias_ref[0, case]
        s_ctx = _dot_nt(qs_scr[qs, :], kc)
        mx = jnp.maximum(jnp.max(s_lat, axis=-1, keepdims=True), jnp.max(s_ctx, axis=-1, keepdims=True))
        p_lat = jnp.exp2(s_lat - mx)
        p_ctx = jnp.exp2(s_ctx - mx)
        den = jnp.sum(p_lat, axis=-1, keepdims=True) + jnp.sum(p_ctx, axis=-1, keepdims=True)
        p = jnp.concatenate([p_lat.astype(BF16), p_ctx.astype(BF16)], axis=1)
        vals = jnp.concatenate([v_ref[0, ks, :], vc], axis=0)
        o_ref[0, qs, :] = (_dot(p, vals) * (1.0 / den)).astype(BF16)


def _na_latent(px, pc, cos_t, sin_t, bias_all, layer, blocks, *, heads, off_q, off_k, off_v):
    b, s, _ = px.shape
    n_ctx = pc.shape[1]
    hd = NA_HEAD_DIM
    col = lambda off: (lambda h, bi: (bi, 0, off // hd + h))
    kern = functools.partial(_na_kernel, blocks=blocks, scale=hd ** -0.5)
    return pl.pallas_call(
        kern,
        grid=(heads, b),
        in_specs=[pl.BlockSpec((1, s, hd), col(off_q)),
                  pl.BlockSpec((1, s, hd), col(off_k)),
                  pl.BlockSpec((1, s, hd), col(off_v)),
                  pl.BlockSpec((1, n_ctx, hd), col(off_k)),
                  pl.BlockSpec((1, n_ctx, hd), col(off_v)),
                  pl.BlockSpec((s, hd), lambda h, bi: (0, 0)),
                  pl.BlockSpec((s, hd), lambda h, bi: (0, 0)),
                  pl.BlockSpec((None, 1) + bias_all.shape[2:], lambda h, bi: (layer, h, 0, 0, 0))],
        out_specs=pl.BlockSpec((1, s, hd), lambda h, bi: (bi, 0, h)),
        out_shape=jax.ShapeDtypeStruct((b, s, heads * hd), BF16),
        scratch_shapes=[pltpu.VMEM((s, hd), BF16)] * 3,
        compiler_params=_params("parallel", "parallel"),
    )(px, px, px, pc, pc, cos_t, sin_t, bias_all)


def _ctx_attn_kernel(q_ref, k_ref, v_ref, o_ref, *, scale):
    s = _dot_nt(q_ref[0], k_ref[0]) * scale
    p = jnp.exp(s - jnp.max(s, axis=-1, keepdims=True))
    den = jnp.sum(p, axis=-1, keepdims=True)
    o_ref[0] = (_dot(p.astype(BF16), v_ref[0]) / den).astype(BF16)


def _ctx_attn(pc, *, heads, off_q, off_k, off_v):
    b, n, _ = pc.shape
    hd = NA_HEAD_DIM
    col = lambda off: (lambda bi, h: (bi, 0, off // hd + h))
    return pl.pallas_call(
        functools.partial(_ctx_attn_kernel, scale=hd ** -0.5),
        grid=(b, heads),
        in_specs=[pl.BlockSpec((1, n, hd), col(off_q)),
                  pl.BlockSpec((1, n, hd), col(off_k)),
                  pl.BlockSpec((1, n, hd), col(off_v))],
        out_specs=pl.BlockSpec((1, n, hd), lambda bi, h: (bi, 0, h)),
        out_shape=jax.ShapeDtypeStruct((b, n, heads * hd), BF16),
        compiler_params=_params("parallel", "parallel"),
    )(pc, pc, pc)


def _fn1_kernel(u_ref, cs_ref, yc_ref, ys_ref, *, groups, gd):
    for g in range(groups):
        sl = slice(g * gd, (g + 1) * gd)
        y = _dot(u_ref[0, :, sl], cs_ref[...])
        yc_ref[0, :, sl] = y[:, :gd].astype(BF16)
        ys_ref[0, :, sl] = y[:, gd:].astype(BF16)


def _fn2_kernel(dc_ref, ds_ref, yc_ref, ys_ref, o_ref):
    o_ref[0] = (_dot(dc_ref[...], yc_ref[0]) + _dot(ds_ref[...], ys_ref[0])).astype(BF16)


def _cos_sin_outer(n):
    r = _tile(n, 64, 32, 16)
    s = jnp.arange(n, dtype=jnp.int32)[None, :]
    unit = 2.0 * math.pi / n
    ang_hi = ((jnp.arange(n // r, dtype=jnp.int32)[:, None] * r * s) % n).astype(F32) * unit
    ang_lo = ((jnp.arange(r, dtype=jnp.int32)[:, None] * s) % n).astype(F32) * unit
    ch, sh = jnp.cos(ang_hi)[:, None, :], jnp.sin(ang_hi)[:, None, :]
    cl, sl = jnp.cos(ang_lo)[None, :, :], jnp.sin(ang_lo)[None, :, :]
    return (ch * cl - sh * sl).reshape(n, n), (sh * cl + ch * sl).reshape(n, n)


def _dft_tables(s, gd):
    cc, sc = _cos_sin_outer(gd)
    cs = jnp.concatenate([cc, sc], axis=1) * (1.0 / math.sqrt(s * gd))
    cp, sp = _cos_sin_outer(s)
    return cs.astype(BF16), cp.astype(BF16), (-sp).astype(BF16)


def _fourier(p3, cs, dc, ds, *, off_u, width):
    b, s, _ = p3.shape
    gd = width // FN_GROUPS
    ts = _tile(s, 1024, 512, 256, 128)
    stage1 = jax.ShapeDtypeStruct((b, s, width), BF16)
    yc, ys = pl.pallas_call(
        functools.partial(_fn1_kernel, groups=FN_GROUPS, gd=gd),
        grid=(b, s // ts),
        in_specs=[pl.BlockSpec((1, ts, width), lambda bi, i: (bi, i, off_u // width)),
                  pl.BlockSpec((gd, 2 * gd), lambda bi, i: (0, 0))],
        out_specs=[pl.BlockSpec((1, ts, width), lambda bi, i: (bi, i, 0))] * 2,
        out_shape=[stage1, stage1],
        compiler_params=_params("parallel", "parallel"),
    )(p3, cs)
    tm = _tile(s, 1024, 512, 256, 128)
    tn = _tile(width, 512, 256, 128)
    dspec = pl.BlockSpec((tm, s), lambda i, bi, j: (i, 0))
    yspec = pl.BlockSpec((1, s, tn), lambda i, bi, j: (bi, 0, j))
    return pl.pallas_call(
        _fn2_kernel,
        grid=(s // tm, b, width // tn),
        in_specs=[dspec, dspec, yspec, yspec],
        out_specs=pl.BlockSpec((1, tm, tn), lambda i, bi, j: (bi, i, j)),
        out_shape=jax.ShapeDtypeStruct((b, s, width), BF16),
        compiler_params=_params("parallel", "parallel", "parallel"),
    )(dc, ds, yc, ys)


def _conv_kernel(x_ref, w_ref, b_ref, o_ref, xp_scr):
    n, tc = x_ref.shape[1], x_ref.shape[2]
    taps = w_ref.shape[0]
    pad = taps // 2
    blk, halo = CONV_ROWS, CONV_HALO
    ext = blk + 2 * halo
    shifts = [k - pad for k in range(taps) if k != pad]
    ri = lax.broadcasted_iota(jnp.int32, (len(shifts) * blk, ext), 0)
    ci = lax.broadcasted_iota(jnp.int32, (len(shifts) * blk, ext), 1)
    want = ri + halo
    for m, sh in enumerate(shifts):
        want = jnp.where((ri >= m * blk) & (ri < (m + 1) * blk), ri - m * blk + halo + sh, want)
    sel = jnp.where(ci == want, 1.0, 0.0).astype(BF16)
    xp_scr[0:halo, :] = jnp.zeros((halo, tc), BF16)
    xp_scr[halo:halo + n, :] = x_ref[0]
    xp_scr[halo + n:2 * halo + n, :] = jnp.zeros((halo, tc), BF16)
    for r in range(n // blk):
        rows = slice(r * blk, (r + 1) * blk)
        moved = _dot(sel, xp_scr[r * blk:r * blk + ext, :])
        acc = b_ref[...] + w_ref[pad:pad + 1, :] * x_ref[0, rows, :].astype(F32)
        for m, sh in enumerate(shifts):
            acc = acc + w_ref[sh + pad:sh + pad + 1, :] * moved[m * blk:(m + 1) * blk]
        o_ref[0, rows, :] = _silu(acc).astype(BF16)


def _conv(p3, conv_w, conv_b, *, off_xbc):
    b, n, _ = p3.shape
    taps, cd = conv_w.shape
    tc = _tile(cd, 512, 256, 128)
    assert off_xbc % tc == 0 and n % CONV_ROWS == 0 and taps // 2 <= CONV_HALO
    return pl.pallas_call(
        _conv_kernel,
        scratch_shapes=[pltpu.VMEM((n + 2 * CONV_HALO, tc), BF16)],
        grid=(b, cd // tc),
        in_specs=[pl.BlockSpec((1, n, tc), lambda bi, j: (bi, 0, off_xbc // tc + j)),
                  pl.BlockSpec((taps, tc), lambda bi, j: (0, j)),
                  pl.BlockSpec((1, tc), lambda bi, j: (0, j))],
        out_specs=pl.BlockSpec((1, n, tc), lambda bi, j: (bi, 0, j)),
        out_shape=jax.ShapeDtypeStruct((b, n, cd), BF16),
        compiler_params=_params("parallel", "parallel"),
    )(p3, conv_w, conv_b.reshape(1, cd))


def _ssd_kernel(xf_ref, xb_ref, dttf_ref, dttb_ref, bcol_ref, acol_ref, dsk_ref, h0_ref,
                yf_ref, yb_ref, hout_ref, st_scr, *, heads, inner, groups, nstate):
    c = pl.program_id(1)
    nc = pl.num_programs(1)
    q = xf_ref.shape[1]
    hp = inner // heads
    hpg = heads // groups
    gw = inner // groups
    lanes = V7X_LANES
    hpt = lanes // hp
    tpg = hpg // hpt

    @pl.when(c == 0)
    def _():
        st_scr[...] = h0_ref[0]

    li = lax.broadcasted_iota(jnp.int32, (q, q), 0)
    si = lax.broadcasted_iota(jnp.int32, (q, q), 1)
    lower = li >= si
    upper = li <= si
    lane_head = lax.broadcasted_iota(jnp.int32, (q, lanes), 1) // hp

    for d, (x_ref, dtt_ref, y_ref) in enumerate(((xf_ref, dttf_ref, yf_ref), (xb_ref, dttb_ref, yb_ref))):
        hs = slice(d * heads, (d + 1) * heads)
        mask = lower if d == 0 else upper
        tri = jnp.where(mask, 1.0, 0.0).astype(BF16)
        dtt = _softplus(dtt_ref[hs, :] + bcol_ref[hs, :])
        a_t = dtt * (acol_ref[hs, :] * LOG2E)
        acum = _dot_sel_nt(tri, a_t)
        acum_t = _dot_sel_l(a_t, jnp.where(upper if d == 0 else lower, 1.0, 0.0).astype(BF16))
        total = acum[q - 1:q, :] if d == 0 else acum[0:1, :]
        total_t = acum_t[:, q - 1:q] if d == 0 else acum_t[:, 0:1]
        w_t = dtt * jnp.exp2(total_t - acum_t)
        row_t = acum_t - jnp.log2(dtt)
        tot = jnp.exp2(total)
        for g in range(groups):
            bg = x_ref[0, :, inner + g * nstate:inner + (g + 1) * nstate]
            cg = x_ref[0, :, inner + (groups + g) * nstate:inner + (groups + g + 1) * nstate]
            cb = _dot_nt(cg, bg)
            bg_t = bg.astype(F32).T
            y_off = _dot(cg, st_scr[d, :, g * gw:(g + 1) * gw].astype(BF16))
            for t in range(tpg):
                h0 = g * hpg + t * hpt
                ls = slice(h0 * hp, h0 * hp + lanes)
                xt = x_ref[0, :, ls]
                lhs_y, lhs_s, rhs_x = [], [], []
                esc = tot_l = None
                for j in range(hpt):
                    h = h0 + j
                    col = acum[:, h:h + 1]
                    lmat = jnp.exp2(jnp.where(mask, col - row_t[h:h + 1, :], MASK_VALUE))
                    lhs_y.append((cb * lmat).astype(BF16))
                    lhs_s.append((bg_t * w_t[h:h + 1, :]).astype(BF16))
                    own = lane_head == j
                    rhs_x.append(jnp.where(own, xt, jnp.zeros_like(xt)))
                    esc_j = jnp.exp2(jnp.broadcast_to(col, (q, lanes)))
                    tot_j = jnp.broadcast_to(tot[:, h:h + 1], (1, lanes))
                    esc = esc_j if j == 0 else jnp.where(own, esc_j, esc)
                    tot_l = tot_j if j == 0 else jnp.where(own[0:1], tot_j, tot_l)
                x_bd = jnp.concatenate(rhs_x, axis=0)
                y = _dot(jnp.concatenate(lhs_y, axis=1), x_bd) + esc * y_off[:, t * lanes:(t + 1) * lanes]
                if d == 0:
                    y = y + dsk_ref[:, ls] * xt.astype(F32)
                y_ref[0, :, ls] = y.astype(BF16)
                st_scr[d, :, ls] = st_scr[d, :, ls] * tot_l + _dot(jnp.concatenate(lhs_s, axis=1), x_bd)

    @pl.when(c == nc - 1)
    def _():
        hout_ref[0] = st_scr[...]


def _ssd(xc, dtt, dt_bias, a_neg, d_e, h0, *, heads, inner):
    b, n, cd = xc.shape
    q = SSM_CHUNK
    assert n % q == 0
    nc = n // q
    groups = SSM_GROUPS
    nstate = (cd - inner) // (2 * groups)
    hp = inner // heads
    assert V7X_LANES % hp == 0 and (heads // groups) % (V7X_LANES // hp) == 0 and nstate <= q
    nh2 = 2 * heads
    bcol = dt_bias.reshape(nh2, 1)
    acol = a_neg.reshape(nh2, 1)
    fwd3 = lambda bi, c: (bi, c, 0)
    bwd3 = lambda bi, c: (bi, nc - 1 - c, 0)
    const2 = lambda bi, c: (0, 0)
    kern = functools.partial(_ssd_kernel, heads=heads, inner=inner, groups=groups, nstate=nstate)
    return pl.pallas_call(
        kern,
        grid=(b, nc),
        in_specs=[pl.BlockSpec((1, q, cd), fwd3),
                  pl.BlockSpec((1, q, cd), bwd3),
                  pl.BlockSpec((nh2, q), lambda bi, c: (0, bi * nc + c)),
                  pl.BlockSpec((nh2, q), lambda bi, c: (0, bi * nc + nc - 1 - c)),
                  pl.BlockSpec((nh2, 1), const2),
                  pl.BlockSpec((nh2, 1), const2),
                  pl.BlockSpec((1, inner), const2),
                  pl.BlockSpec((1, 2, nstate, inner), lambda bi, c: (bi, 0, 0, 0))],
        out_specs=[pl.BlockSpec((1, q, inner), fwd3),
                   pl.BlockSpec((1, q, inner), bwd3),
                   pl.BlockSpec((1, 2, nstate, inner), lambda bi, c: (bi, 0, 0, 0))],
        out_shape=[jax.ShapeDtypeStruct((b, n, inner), BF16),
                   jax.ShapeDtypeStruct((b, n, inner), BF16),
                   jax.ShapeDtypeStruct((b, 2, nstate, inner), F32)],
        scratch_shapes=[pltpu.VMEM((2, nstate, inner), F32)],
        compiler_params=_params("parallel", "arbitrary"),
    )(xc, xc, dtt, dtt, bcol, acol, d_e, h0)


def _mix_kernel(ona_ref, ofn_ref, yf_ref, yb_ref, z_ref, gates_ref, x_ref, g1_ref, nw_ref,
                w1_ref, w2_ref, w3_ref, wo_ref, lg_ref, lb_ref, o_ref, *, groups, alpha):
    d = o_ref.shape[1]
    gy = (yf_ref[...].astype(F32) + yb_ref[...].astype(F32)) * _silu(z_ref[...].astype(F32))
    gw = gy.shape[1] // groups
    parts = []
    for g in range(groups):
        sl = slice(g * gw, (g + 1) * gw)
        v = gy[:, sl]
        ms = jnp.mean(v * v, axis=-1, keepdims=True)
        parts.append((v * lax.rsqrt(ms + RMS_EPS) * nw_ref[:, sl]).astype(BF16))
    o_ssm = jnp.concatenate(parts, axis=1)
    gate = lambda k: _sigmoid(gates_ref[:, k * d:(k + 1) * d].astype(F32))
    m = (gate(0) * _dot(ona_ref[...], w1_ref[...])
         + gate(1) * _dot(ofn_ref[...], w2_ref[...])
         + gate(2) * _dot(o_ssm, w3_ref[...]))
    mix = _dot(m.astype(BF16), wo_ref[...])
    y = alpha * x_ref[...] + g1_ref[0, 0] * mix
    o_ref[...] = _ln(y) * lg_ref[...] + lb_ref[...]


def _mix(x2d, o_na, o_fn, y_f, y_b, p2, mods, norm_w, w1, w2, w3, wo, ln_g, ln_b,
         *, off_z, off_g, inner, alpha):
    m, d = x2d.shape
    na = o_na.shape[1]
    fn = o_fn.shape[1]
    tm = _tile(m, 256, 128)
    assert off_z % inner == 0 and off_g % (3 * d) == 0
    layer = mods.layer
    row = lambda w, blk=0: pl.BlockSpec((tm, w), lambda i: (i, blk))
    full = lambda r, c: pl.BlockSpec((r, c), lambda i: (0, 0))
    stacked = lambda r, c: pl.BlockSpec((None, r, c), lambda i: (layer, 0, 0))
    return pl.pallas_call(
        functools.partial(_mix_kernel, groups=SSM_GROUPS, alpha=alpha),
        grid=(m // tm,),
        in_specs=[row(na), row(fn), row(inner), row(inner),
                  row(inner, off_z // inner), row(3 * d, off_g // (3 * d)), row(d),
                  mods.spec(2, tm),
                  full(1, inner), stacked(na, d), stacked(fn, d), stacked(inner, d), stacked(d, d),
                  full(1, d), full(1, d)],
        out_specs=pl.BlockSpec((tm, d), lambda i: (i, 0)),
        out_shape=jax.ShapeDtypeStruct((m, d), F32),
        compiler_params=_params("parallel"),
    )(o_na, o_fn, y_f, y_b, p2, p2, x2d, mods.table, norm_w, w1, w2, w3, wo,
      ln_g.reshape(1, d), ln_b.reshape(1, d))


def _ffn_kernel(x_ref, sh_ref, sc_ref, g_ref, wg_ref, wu_ref, wd_ref, lg_ref, lb_ref, o_ref,
                h_scr, acc_scr, *, alpha):
    j = pl.program_id(1)
    last = pl.num_programs(1) - 1
    slabs = _slabs(x_ref.shape[0])

    def down(h):
        a = (_silu(_dot(h, wg_ref[...])) * _dot(h, wu_ref[...])).astype(BF16)
        return _dot(a, wd_ref[...])

    @pl.when(j == 0)
    def _():
        for rows in slabs:
            h = (_ln(x_ref[rows, :]) * (1.0 + sc_ref[0, 0]) + sh_ref[0, 0]).astype(BF16)
            h_scr[rows, :] = h
            acc_scr[rows, :] = down(h)

    @pl.when((j > 0) & (j < last))
    def _():
        acc_scr[...] += down(h_scr[...])

    @pl.when(j == last)
    def _():
        for rows in slabs:
            y = alpha * x_ref[rows, :] + g_ref[0, 0] * (acc_scr[rows, :] + down(h_scr[rows, :]))
            o_ref[rows, :] = _ln(y) * lg_ref[...] + lb_ref[...]


def _ffn(x2d, mods, wg, wu, wd, ln_g, ln_b, alpha):
    m, d = x2d.shape
    hid = wg.shape[2]
    tm = _tile(m, 512, 256, 128)
    th = _tile(hid, 512, 256, 128)
    assert hid // th >= 2
    layer = mods.layer
    vec = pl.BlockSpec((1, d), lambda i, j: (0, 0))
    return pl.pallas_call(
        functools.partial(_ffn_kernel, alpha=alpha),
        grid=(m // tm, hid // th),
        in_specs=[pl.BlockSpec((tm, d), lambda i, j: (i, 0)),
                  mods.spec(3, tm), mods.spec(4, tm), mods.spec(5, tm),
                  pl.BlockSpec((None, d, th), lambda i, j: (layer, 0, j)),
                  pl.BlockSpec((None, d, th), lambda i, j: (layer, 0, j)),
                  pl.BlockSpec((None, th, d), lambda i, j: (layer, j, 0)),
                  vec, vec],
        out_specs=pl.BlockSpec((tm, d), lambda i, j: (i, 0)),
        out_shape=jax.ShapeDtypeStruct((m, d), F32),
        scratch_shapes=[pltpu.VMEM((tm, d), BF16), pltpu.VMEM((tm, d), F32)],
        compiler_params=_params("parallel", "arbitrary"),
    )(x2d, mods.table, mods.table, mods.table, wg, wu, wd, ln_g.reshape(1, d), ln_b.reshape(1, d))


def _rope_tables(n_tokens):
    t = np.arange(n_tokens)
    half = NA_HEAD_DIM // 4
    inv = ROPE_THETA ** (-np.arange(half, dtype=np.float32) / half)
    pos = np.stack([t // GRID_W, t % GRID_W], axis=1).astype(np.float32)
    lane = np.arange(NA_HEAD_DIM)
    ang = pos[:, lane // (2 * half)] * inv[lane % half][None, :]
    sign = np.where((lane % (2 * half)) < half, -1.0, 1.0).astype(np.float32)
    return jnp.asarray(np.cos(ang), F32), jnp.asarray(np.sin(ang) * sign, F32)


def _bias_tables(rpb, cases):
    kc = NA_WIN_COLS
    col = np.arange(GRID_W)
    c_start = np.clip(col - kc // 2, 0, GRID_W - kc)
    col_in = (col[None, :] >= c_start[:, None]) & (col[None, :] < c_start[:, None] + kc)
    dc = np.clip(col[None, :] - col[:, None], -(kc - 1), kc - 1) + (kc - 1)
    depth, heads, n_dr, n_dc = rpb.shape
    onehot = (dc[None] == np.arange(n_dc)[:, None, None]).astype(np.float32)
    t = jnp.einsum('lhrc,cqk->lhrqk', rpb.astype(F32) * LOG2E, jnp.asarray(onehot),
                   precision=lax.Precision.HIGHEST)
    t = jnp.where(col_in, t, MASK_VALUE)
    t = jnp.concatenate([t, jnp.full((depth, heads, 1, GRID_W, GRID_W), MASK_VALUE, F32)], axis=2)
    plan = np.full((len(cases), NA_QROWS, NA_KROWS), n_dr, np.int32)
    for ci, (rel_rs, off) in enumerate(cases):
        for qr in range(NA_QROWS):
            for kr in range(NA_KROWS):
                if rel_rs[qr] <= kr < rel_rs[qr] + NA_WIN_ROWS:
                    plan[ci, qr, kr] = kr + off - qr + NA_WIN_ROWS - 1
    g = t[:, :, plan]
    g = jnp.transpose(g, (0, 1, 2, 3, 5, 4, 6))
    return g.reshape(depth, heads, len(cases), NA_QROWS * GRID_W, NA_KROWS * GRID_W)


def kernel(x, c, ctx, c_ctx, w_mod, w_in, na_rpb, ssm_conv_w, ssm_conv_b, ssm_a_log, ssm_dt_bias, ssm_d, ssm_norm_w, w_br_na, w_br_fn, w_br_ssm, w_out, ln1_g, ln1_b, w_ffn_gate, w_ffn_up, w_ffn_down, ln2_g, ln2_b):
    b, s, d = x.shape
    n_ctx = ctx.shape[1]
    depth = w_mod.shape[0]
    na = w_br_na.shape[1]
    fn = w_br_fn.shape[1]
    inner = w_br_ssm.shape[1]
    conv_dim = ssm_conv_w.shape[2]
    ssm_heads = ssm_d.shape[1]
    na_heads = na_rpb.shape[1]
    assert na == na_heads * NA_HEAD_DIM and inner == ssm_heads * SSM_HEAD_DIM and s % GRID_W == 0
    alpha = (2.0 * depth) ** 0.25

    c_u = 3 * na + fn
    c_z = c_u + conv_dim
    c_dt = c_z + inner
    c_g = c_dt + 2 * ssm_heads
    off_q, off_k, off_v, off_u = 0, na, 2 * na, 3 * na
    off_z = c_u
    off_g = off_z + inner
    off_xbc = off_g + 3 * d
    assert off_u % fn == 0

    rows = b + 1
    rows_pad = -(-rows // MOD_ROWS_ALIGN) * MOD_ROWS_ALIGN
    c_rows = jnp.concatenate([c, c_ctx[None, :], jnp.zeros((rows_pad - rows, d), F32)], axis=0)
    mod_table = _mods(c_rows, w_mod).reshape(depth, rows_pad, 1, 6 * d)

    cos_t, sin_t = _rope_tables(s)
    na_blocks, na_cases = _na_plan(s // GRID_W)
    dft_x = _dft_tables(s, fn // FN_GROUPS)
    dft_c = _dft_tables(n_ctx, fn // FN_GROUPS)
    nstate = (conv_dim - inner) // (2 * SSM_GROUPS)
    zero_state = jnp.zeros((b, 2, nstate, inner), F32)

    x2 = x.reshape(b * s, d)
    c2 = ctx.reshape(b * n_ctx, d)
    flat = lambda t: t.reshape(-1, t.shape[-1])
    n_main = off_xbc + conv_dim
    w_main = _relayout_w_in(jnp.swapaxes(w_in, 1, 2),
                            [(0, c_u), (c_z, inner), (c_g, 3 * d), (c_u, conv_dim), (c_dt, RELAYOUT_TILE)])
    w_dtt_all = w_main[:, n_main:n_main + 2 * ssm_heads, :]
    w1, w2, w3 = w_br_na.astype(BF16), w_br_fn.astype(BF16), w_br_ssm.astype(BF16)
    wo = w_out.astype(BF16)
    wg, wu, wd = w_ffn_gate.astype(BF16), w_ffn_up.astype(BF16), w_ffn_down.astype(BF16)
    bias_all = _bias_tables(na_rpb, na_cases)
    mix_kw = dict(off_z=off_z, off_g=off_g, inner=inner, alpha=alpha)
    for l in range(depth):
        with_ctx = l < depth - 1
        w_dtt = w_dtt_all[l]
        mx = _Mods(mod_table, l, 0, s)
        mc = _Mods(mod_table, l, b, b * n_ctx)
        a_neg = -jnp.exp(ssm_a_log[l].astype(F32))
        d_e = jnp.repeat(ssm_d[l].astype(F32), SSM_HEAD_DIM).reshape(1, inner)
        norm_w = ssm_norm_w[l].astype(F32).reshape(1, inner)

        px, dttx = _inproj(x2, mx, w_main, n_main, w_dtt)
        pc, dttc = _inproj(c2, mc, w_main, n_main, w_dtt)
        px3 = px.reshape(b, s, -1)
        pc3 = pc.reshape(b, n_ctx, -1)

        o_na = _na_latent(px3, pc3, cos_t, sin_t, bias_all, l, na_blocks, heads=na_heads,
                          off_q=off_q, off_k=off_k, off_v=off_v)
        o_fn = _fourier(px3, *dft_x, off_u=off_u, width=fn)
        xc_c = _conv(pc3, ssm_conv_w[l], ssm_conv_b[l], off_xbc=off_xbc)
        yf_c, yb_c, h_ctx = _ssd(xc_c, dttc, ssm_dt_bias[l], a_neg, d_e, zero_state, heads=ssm_heads, inner=inner)
        xc_x = _conv(px3, ssm_conv_w[l], ssm_conv_b[l], off_xbc=off_xbc)
        yf_x, yb_x, _ = _ssd(xc_x, dttx, ssm_dt_bias[l], a_neg, d_e, h_ctx, heads=ssm_heads, inner=inner)

        x2 = _mix(x2, flat(o_na), flat(o_fn), flat(yf_x), flat(yb_x), px, mx, norm_w,
                  w1, w2, w3, wo, ln1_g[l], ln1_b[l], **mix_kw)
        x2 = _ffn(x2, mx, wg, wu, wd, ln2_g[l], ln2_b[l], alpha)
        if with_ctx:
            o_na_c = _ctx_attn(pc3, heads=na_heads, off_q=off_q, off_k=off_k, off_v=off_v)
            o_fn_c = _fourier(pc3, *dft_c, off_u=off_u, width=fn)
            c2 = _mix(c2, flat(o_na_c), flat(o_fn_c), flat(yf_c), flat(yb_c), pc, mc, norm_w,
                      w1, w2, w3, wo, ln1_g[l], ln1_b[l], **mix_kw)
            c2 = _ffn(c2, mc, wg, wu, wd, ln2_g[l], ln2_b[l], alpha)
    return x2.reshape(b, s, d)
```

```python
import functools
import math

import jax
import jax.numpy as jnp
import numpy as np
from jax import lax
from jax.experimental import pallas as pl
from jax.experimental.pallas import tpu as pltpu

F32 = jnp.float32
BF16 = jnp.bfloat16

GRID_W = 64
NA_WIN_ROWS = 8
NA_WIN_COLS = 16
NA_HEAD_DIM = 128
ROPE_THETA = 10000.0
FN_GROUPS = 4
SSM_GROUPS = 4
SSM_HEAD_DIM = 64
SSM_CHUNK = 128
LN_EPS = 1e-6
RMS_EPS = 1e-5
MASK_VALUE = -1e30
LOG2E = 1.0 / math.log(2.0)

NA_QROWS = 4
NA_KROWS = NA_WIN_ROWS + NA_QROWS

V7X_VMEM_BYTES = 64 * 1024 * 1024
VMEM_LIMIT_BYTES = V7X_VMEM_BYTES - 8 * 1024 * 1024
V7X_LANES = 128
V7X_SUBLANES = 8
MOD_ROWS_ALIGN = 16
RELAYOUT_TILE = 4 * V7X_LANES
CONV_ROWS = 128
CONV_HALO = 16
ROW_SLAB = 256


def _params(*sem):
    return pltpu.CompilerParams(dimension_semantics=sem, vmem_limit_bytes=VMEM_LIMIT_BYTES)


def _tile(n, *candidates):
    for c in candidates:
        if n % c == 0:
            return c
    return n


def _sigmoid(x):
    return 1.0 / (1.0 + jnp.exp(-x))


def _silu(x):
    return x * _sigmoid(x)


def _softplus(x):
    return jnp.maximum(x, 0.0) + jnp.log(1.0 + jnp.exp(-jnp.abs(x)))


def _ln(x):
    mu = jnp.mean(x, axis=-1, keepdims=True)
    xc = x - mu
    var = jnp.mean(xc * xc, axis=-1, keepdims=True)
    return xc * lax.rsqrt(var + LN_EPS)


def _dot(a, b):
    return jnp.dot(a, b, preferred_element_type=F32)


def _dot_nt(a, b):
    return lax.dot_general(a, b, (((1,), (1,)), ((), ())), preferred_element_type=F32)


def _split3(a):
    hi = a.astype(BF16)
    r = a - hi.astype(F32)
    mid = r.astype(BF16)
    lo = (r - mid.astype(F32)).astype(BF16)
    return hi, mid, lo


def _dot_sel_l(a, sel):
    hi, mid, lo = _split3(a)
    return _dot(hi, sel) + _dot(mid, sel) + _dot(lo, sel)


def _dot_sel_nt(sel, a):
    hi, mid, lo = _split3(a)
    return _dot_nt(sel, hi) + _dot_nt(sel, mid) + _dot_nt(sel, lo)


def _mods_kernel(c_ref, w_ref, o_ref):
    a = _silu(c_ref[...]).astype(BF16)
    o_ref[0] = _dot(a, w_ref[0].astype(BF16))


def _mods(c_rows, w_mod):
    depth, d, n = w_mod.shape
    r = c_rows.shape[0]
    tn = _tile(n, 1024, 512, 256, 128)
    return pl.pallas_call(
        _mods_kernel,
        grid=(depth, n // tn),
        in_specs=[pl.BlockSpec((r, d), lambda l, j: (0, 0)),
                  pl.BlockSpec((1, d, tn), lambda l, j: (l, 0, j))],
        out_specs=pl.BlockSpec((1, r, tn), lambda l, j: (l, 0, j)),
        out_shape=jax.ShapeDtypeStruct((depth, r, n), F32),
        compiler_params=_params("parallel", "parallel"),
    )(c_rows, w_mod)


def _relayout_kernel(at_ref, bt_ref, a_ref, b_ref, o_ref, *, first, stop, shift):
    j = pl.program_id(1)
    shifted = (j >= first) & (j < stop)

    @pl.when(shifted)
    def _():
        keep = a_ref.shape[0] - shift
        o_ref[:keep, :] = a_ref[shift:, :].astype(BF16)
        o_ref[keep:, :] = b_ref[...].astype(BF16)

    @pl.when(jnp.logical_not(shifted))
    def _():
        o_ref[...] = a_ref[...].astype(BF16)


def _relayout_w_in(w_in_t, segments):
    depth, n_src, d = w_in_t.shape
    tw = RELAYOUT_TILE
    n_dst = sum(w for _, w in segments)
    src_tile, shift, first, stop = [], 0, 0, 0
    for src, width in segments:
        assert width % tw == 0
        if src % tw:
            assert shift == 0 and (src % tw) % 16 == 0 and tw % (src % tw) == 0
            shift, first, stop = src % tw, len(src_tile), len(src_tile) + width // tw
        src_tile += [src // tw + t for t in range(width // tw)]
        assert src + width <= n_src
    a_tile = jnp.asarray(src_tile, jnp.int32)
    hb = shift if shift else 16
    b_tile = jnp.asarray([(t + 1) * tw // hb if first <= k < stop else 0 for k, t in enumerate(src_tile)], jnp.int32)
    return pl.pallas_call(
        functools.partial(_relayout_kernel, first=first, stop=stop, shift=shift),
        grid_spec=pltpu.PrefetchScalarGridSpec(
            num_scalar_prefetch=2,
            grid=(depth, n_dst // tw),
            in_specs=[pl.BlockSpec((None, tw, d), lambda l, j, at, bt: (l, at[j], 0)),
                      pl.BlockSpec((None, hb, d), lambda l, j, at, bt: (l, bt[j], 0))],
            out_specs=pl.BlockSpec((None, tw, d), lambda l, j, at, bt: (l, j, 0))),
        out_shape=jax.ShapeDtypeStruct((depth, n_dst, d), BF16),
        compiler_params=_params("parallel", "parallel"),
    )(a_tile, b_tile, w_in_t, w_in_t)


def _slabs(tm):
    step = _tile(tm, ROW_SLAB)
    return [slice(r, r + step) for r in range(0, tm, step)]


def _inproj_kernel(x_ref, sh_ref, sc_ref, w_ref, wdtt_ref, o_ref, dtt_ref, h_scr):
    j = pl.program_id(1)

    @pl.when(j == 0)
    def _():
        slabs = _slabs(x_ref.shape[0])

        def norm(rows):
            h = (_ln(x_ref[rows, :]) * (1.0 + sc_ref[0, 0]) + sh_ref[0, 0]).astype(BF16)
            h_scr[rows, :] = h
            return h

        ahead = norm(slabs[0])
        for r, rows in enumerate(slabs):
            h = ahead
            if r + 1 < len(slabs):
                ahead = norm(slabs[r + 1])
            dtt_ref[:, rows] = _dot_nt(wdtt_ref[...], h)
            o_ref[rows, :] = _dot_nt(h, w_ref[...]).astype(BF16)

    @pl.when(j > 0)
    def _():
        o_ref[...] = _dot_nt(h_scr[...], w_ref[...]).astype(BF16)


class _Mods:
    def __init__(self, table, layer, row0, rows_per_mod):
        self.table, self.layer, self.row0, self.rows_per_mod = table, layer, row0, rows_per_mod
        self.d = table.shape[-1] // 6

    def spec(self, chunk, tm):
        assert self.rows_per_mod % tm == 0
        layer, row0, rpm = self.layer, self.row0, self.rows_per_mod
        return pl.BlockSpec((1, 1, 1, self.d), lambda i, *_: (layer, row0 + i * tm // rpm, 0, chunk))


def _inproj(x2d, mods, w_main, n, w_dtt):
    m, d = x2d.shape
    ndt = w_dtt.shape[0]
    tm = _tile(m, 1024, 512, 256, 128)
    tn = _tile(n, 2560, 1536, 1024, 512, 256, 128)
    layer = mods.layer
    return pl.pallas_call(
        _inproj_kernel,
        grid=(m // tm, n // tn),
        in_specs=[pl.BlockSpec((tm, d), lambda i, j: (i, 0)),
                  mods.spec(0, tm), mods.spec(1, tm),
                  pl.BlockSpec((None, tn, d), lambda i, j: (layer, j, 0)),
                  pl.BlockSpec((ndt, d), lambda i, j: (0, 0))],
        out_specs=[pl.BlockSpec((tm, tn), lambda i, j: (i, j)),
                   pl.BlockSpec((ndt, tm), lambda i, j: (0, i))],
        out_shape=[jax.ShapeDtypeStruct((m, n), BF16),
                   jax.ShapeDtypeStruct((ndt, m), F32)],
        scratch_shapes=[pltpu.VMEM((tm, d), BF16)],
        compiler_params=_params("parallel", "arbitrary"),
    )(x2d, mods.table, mods.table, w_main, w_dtt)


def _na_plan(rows):
    assert rows % NA_QROWS == 0 and rows >= NA_KROWS
    blocks, cases = [], []
    for r0 in range(0, rows, NA_QROWS):
        start = min(max(r0 - NA_WIN_ROWS // 2, 0), rows - NA_KROWS)
        rs = [min(max(r - NA_WIN_ROWS // 2, 0), rows - NA_WIN_ROWS) for r in range(r0, r0 + NA_QROWS)]
        case = (tuple(v - start for v in rs), start - r0)
        if case not in cases:
            cases.append(case)
        blocks.append((r0, start, cases.index(case)))
    return blocks, cases


def _na_kernel(q_ref, k_ref, v_ref, kc_ref, vc_ref, cos_ref, sin_ref, bias_ref, o_ref,
               qr_scr, qs_scr, kr_scr, *, blocks, scale):
    s, hd = qr_scr.shape
    half = hd // 4
    src = lax.broadcasted_iota(jnp.int32, (hd, hd), 0)
    dst = lax.broadcasted_iota(jnp.int32, (hd, hd), 1)
    partner = jnp.where((dst % (2 * half)) < half, dst + half, dst - half)
    perm = jnp.where(src == partner, 1.0, 0.0).astype(BF16)

    def rope(t):
        return t.astype(F32) * cos_ref[...] + _dot(t, perm) * sin_ref[...]

    qscale = scale * LOG2E
    qr_scr[...] = (rope(q_ref[0]) * qscale).astype(BF16)
    qs_scr[...] = (q_ref[0].astype(F32) * qscale).astype(BF16)
    kr_scr[...] = rope(k_ref[0]).astype(BF16)
    kc = kc_ref[0]
    vc = vc_ref[0]
    nq = NA_QROWS * GRID_W
    nk = NA_KROWS * GRID_W

    def scores(block):
        r0, start, case = block
        qs = slice(r0 * GRID_W, r0 * GRID_W + nq)
        ks = slice(start * GRID_W, start * GRID_W + nk)
        return _dot_nt(qr_scr[qs, :], kr_scr[ks, :]) + bias_ref[0, case], _dot_nt(qs_scr[qs, :], kc)

    ahead = scores(blocks[0])
    for i, (r0, start, case) in enumerate(blocks):
        qs = slice(r0 * GRID_W, r0 * GRID_W + nq)
        ks = slice(start * GRID_W, start * GRID_W + nk)
        s_lat, s_ctx = ahead
        if i + 1 < len(blocks):
            ahead = scores(blocks[i + 1])
        mx = jnp.maximum(jnp.max(s_lat, axis=-1, keepdims=True), jnp.max(s_ctx, axis=-1, keepdims=True))
        p_lat = jnp.exp2(s_lat - mx)
        p_ctx = jnp.exp2(s_ctx - mx)
        den = jnp.sum(p_lat, axis=-1, keepdims=True) + jnp.sum(p_ctx, axis=-1, keepdims=True)
        p = jnp.concatenate([p_lat.astype(BF16), p_ctx.astype(BF16)], axis=1)
        vals = jnp.concatenate([v_ref[0, ks, :], vc], axis=0)
        o_ref[0, qs, :] = (_dot(p, vals) * (1.0 / den)).astype(BF16)


def _na_latent(px, pc, cos_t, sin_t, bias_all, layer, blocks, *, heads, off_q, off_k, off_v):
    b, s, _ = px.shape
    n_ctx = pc.shape[1]
    hd = NA_HEAD_DIM
    col = lambda off: (lambda h, bi: (bi, 0, off // hd + h))
    kern = functools.partial(_na_kernel, blocks=blocks, scale=hd ** -0.5)
    return pl.pallas_call(
        kern,
        grid=(heads, b),
        in_specs=[pl.BlockSpec((1, s, hd), col(off_q)),
                  pl.BlockSpec((1, s, hd), col(off_k)),
                  pl.BlockSpec((1, s, hd), col(off_v)),
                  pl.BlockSpec((1, n_ctx, hd), col(off_k)),
                  pl.BlockSpec((1, n_ctx, hd), col(off_v)),
                  pl.BlockSpec((s, hd), lambda h, bi: (0, 0)),
                  pl.BlockSpec((s, hd), lambda h, bi: (0, 0)),
                  pl.BlockSpec((None, 1) + bias_all.shape[2:], lambda h, bi: (layer, h, 0, 0, 0))],
        out_specs=pl.BlockSpec((1, s, hd), lambda h, bi: (bi, 0, h)),
        out_shape=jax.ShapeDtypeStruct((b, s, heads * hd), BF16),
        scratch_shapes=[pltpu.VMEM((s, hd), BF16)] * 3,
        compiler_params=_params("parallel", "parallel"),
    )(px, px, px, pc, pc, cos_t, sin_t, bias_all)


def _ctx_attn_kernel(q_ref, k_ref, v_ref, o_ref, *, scale):
    s = _dot_nt(q_ref[0], k_ref[0]) * scale
    p = jnp.exp(s - jnp.max(s, axis=-1, keepdims=True))
    den = jnp.sum(p, axis=-1, keepdims=True)
    o_ref[0] = (_dot(p.astype(BF16), v_ref[0]) / den).astype(BF16)


def _ctx_attn(pc, *, heads, off_q, off_k, off_v):
    b, n, _ = pc.shape
    hd = NA_HEAD_DIM
    col = lambda off: (lambda bi, h: (bi, 0, off // hd + h))
    return pl.pallas_call(
        functools.partial(_ctx_attn_kernel, scale=hd ** -0.5),
        grid=(b, heads),
        in_specs=[pl.BlockSpec((1, n, hd), col(off_q)),
                  pl.BlockSpec((1, n, hd), col(off_k)),
                  pl.BlockSpec((1, n, hd), col(off_v))],
        out_specs=pl.BlockSpec((1, n, hd), lambda bi, h: (bi, 0, h)),
        out_shape=jax.ShapeDtypeStruct((b, n, heads * hd), BF16),
        compiler_params=_params("parallel", "parallel"),
    )(pc, pc, pc)


def _fn1_kernel(u_ref, cs_ref, yc_ref, ys_ref, *, groups, gd):
    for g in range(groups):
        sl = slice(g * gd, (g + 1) * gd)
        y = _dot(u_ref[0, :, sl], cs_ref[...])
        yc_ref[0, :, sl] = y[:, :gd].astype(BF16)
        ys_ref[0, :, sl] = y[:, gd:].astype(BF16)


def _fn2_kernel(dc_ref, ds_ref, yc_ref, ys_ref, o_ref):
    o_ref[0] = (_dot(dc_ref[...], yc_ref[0]) + _dot(ds_ref[...], ys_ref[0])).astype(BF16)


def _cos_sin_outer(n):
    r = _tile(n, 64, 32, 16)
    s = jnp.arange(n, dtype=jnp.int32)[None, :]
    unit = 2.0 * math.pi / n
    ang_hi = ((jnp.arange(n // r, dtype=jnp.int32)[:, None] * r * s) % n).astype(F32) * unit
    ang_lo = ((jnp.arange(r, dtype=jnp.int32)[:, None] * s) % n).astype(F32) * unit
    ch, sh = jnp.cos(ang_hi)[:, None, :], jnp.sin(ang_hi)[:, None, :]
    cl, sl = jnp.cos(ang_lo)[None, :, :], jnp.sin(ang_lo)[None, :, :]
    return (ch * cl - sh * sl).reshape(n, n), (sh * cl + ch * sl).reshape(n, n)


def _dft_tables(s, gd):
    cc, sc = _cos_sin_outer(gd)
    cs = jnp.concatenate([cc, sc], axis=1) * (1.0 / math.sqrt(s * gd))
    cp, sp = _cos_sin_outer(s)
    return cs.astype(BF16), cp.astype(BF16), (-sp).astype(BF16)


def _fourier(p3, cs, dc, ds, *, off_u, width):
    b, s, _ = p3.shape
    gd = width // FN_GROUPS
    ts = _tile(s, 1024, 512, 256, 128)
    stage1 = jax.ShapeDtypeStruct((b, s, width), BF16)
    yc, ys = pl.pallas_call(
        functools.partial(_fn1_kernel, groups=FN_GROUPS, gd=gd),
        grid=(b, s // ts),
        in_specs=[pl.BlockSpec((1, ts, width), lambda bi, i: (bi, i, off_u // width)),
                  pl.BlockSpec((gd, 2 * gd), lambda bi, i: (0, 0))],
        out_specs=[pl.BlockSpec((1, ts, width), lambda bi, i: (bi, i, 0))] * 2,
        out_shape=[stage1, stage1],
        compiler_params=_params("parallel", "parallel"),
    )(p3, cs)
    tm = _tile(s, 1024, 512, 256, 128)
    tn = _tile(width, 512, 256, 128)
    dspec = pl.BlockSpec((tm, s), lambda i, bi, j: (i, 0))
    yspec = pl.BlockSpec((1, s, tn), lambda i, bi, j: (bi, 0, j))
    return pl.pallas_call(
        _fn2_kernel,
        grid=(s // tm, b, width // tn),
        in_specs=[dspec, dspec, yspec, yspec],
        out_specs=pl.BlockSpec((1, tm, tn), lambda i, bi, j: (bi, i, j)),
        out_shape=jax.ShapeDtypeStruct((b, s, width), BF16),
        compiler_params=_params("parallel", "parallel", "parallel"),
    )(dc, ds, yc, ys)


def _conv_kernel(x_ref, w_ref, b_ref, o_ref, xp_scr):
    n, tc = x_ref.shape[1], x_ref.shape[2]
    taps = w_ref.shape[0]
    pad = taps // 2
    blk, halo = CONV_ROWS, CONV_HALO
    ext = blk + 2 * halo
    shifts = [k - pad for k in range(taps) if k != pad]
    ri = lax.broadcasted_iota(jnp.int32, (len(shifts) * blk, ext), 0)
    ci = lax.broadcasted_iota(jnp.int32, (len(shifts) * blk, ext), 1)
    want = ri + halo
    for m, sh in enumerate(shifts):
        want = jnp.where((ri >= m * blk) & (ri < (m + 1) * blk), ri - m * blk + halo + sh, want)
    sel = jnp.where(ci == want, 1.0, 0.0).astype(BF16)
    xp_scr[0:halo, :] = jnp.zeros((halo, tc), BF16)
    xp_scr[halo:halo + n, :] = x_ref[0]
    xp_scr[halo + n:2 * halo + n, :] = jnp.zeros((halo, tc), BF16)
    for r in range(n // blk):
        rows = slice(r * blk, (r + 1) * blk)
        moved = _dot(sel, xp_scr[r * blk:r * blk + ext, :])
        acc = b_ref[...] + w_ref[pad:pad + 1, :] * x_ref[0, rows, :].astype(F32)
        for m, sh in enumerate(shifts):
            acc = acc + w_ref[sh + pad:sh + pad + 1, :] * moved[m * blk:(m + 1) * blk]
        o_ref[0, rows, :] = _silu(acc).astype(BF16)


def _conv(p3, conv_w, conv_b, *, off_xbc):
    b, n, _ = p3.shape
    taps, cd = conv_w.shape
    tc = _tile(cd, 512, 256, 128)
    assert off_xbc % tc == 0 and n % CONV_ROWS == 0 and taps // 2 <= CONV_HALO
    return pl.pallas_call(
        _conv_kernel,
        scratch_shapes=[pltpu.VMEM((n + 2 * CONV_HALO, tc), BF16)],
        grid=(b, cd // tc),
        in_specs=[pl.BlockSpec((1, n, tc), lambda bi, j: (bi, 0, off_xbc // tc + j)),
                  pl.BlockSpec((taps, tc), lambda bi, j: (0, j)),
                  pl.BlockSpec((1, tc), lambda bi, j: (0, j))],
        out_specs=pl.BlockSpec((1, n, tc), lambda bi, j: (bi, 0, j)),
        out_shape=jax.ShapeDtypeStruct((b, n, cd), BF16),
        compiler_params=_params("parallel", "parallel"),
    )(p3, conv_w, conv_b.reshape(1, cd))


def _ssd_kernel(xf_ref, xb_ref, dttf_ref, dttb_ref, bcol_ref, acol_ref, dsk_ref, h0_ref,
                yf_ref, yb_ref, hout_ref, st_scr, *, heads, inner, groups, nstate):
    c = pl.program_id(1)
    nc = pl.num_programs(1)
    q = xf_ref.shape[1]
    hp = inner // heads
    hpg = heads // groups
    gw = inner // groups
    lanes = V7X_LANES
    hpt = lanes // hp
    tpg = hpg // hpt

    @pl.when(c == 0)
    def _():
        st_scr[...] = h0_ref[0]

    li = lax.broadcasted_iota(jnp.int32, (q, q), 0)
    si = lax.broadcasted_iota(jnp.int32, (q, q), 1)
    lower = li >= si
    upper = li <= si
    lane_head = lax.broadcasted_iota(jnp.int32, (q, lanes), 1) // hp

    for d, (x_ref, dtt_ref, y_ref) in enumerate(((xf_ref, dttf_ref, yf_ref), (xb_ref, dttb_ref, yb_ref))):
        hs = slice(d * heads, (d + 1) * heads)
        mask = lower if d == 0 else upper
        tri = jnp.where(mask, 1.0, 0.0).astype(BF16)
        dtt = _softplus(dtt_ref[hs, :] + bcol_ref[hs, :])
        a_t = dtt * (acol_ref[hs, :] * LOG2E)
        acum = _dot_sel_nt(tri, a_t)
        acum_t = _dot_sel_l(a_t, jnp.where(upper if d == 0 else lower, 1.0, 0.0).astype(BF16))
        total = acum[q - 1:q, :] if d == 0 else acum[0:1, :]
        total_t = acum_t[:, q - 1:q] if d == 0 else acum_t[:, 0:1]
        w_t = dtt * jnp.exp2(total_t - acum_t)
        row_t = acum_t - jnp.log2(dtt)
        tot = jnp.exp2(total)
        group_vals = {}

        def group(g):
            if g not in group_vals:
                bg = x_ref[0, :, inner + g * nstate:inner + (g + 1) * nstate]
                cg = x_ref[0, :, inner + (groups + g) * nstate:inner + (groups + g + 1) * nstate]
                cb = _dot_nt(cg, bg)
                bg_t = bg.astype(F32).T
                y_off = _dot(cg, st_scr[d, :, g * gw:(g + 1) * gw].astype(BF16))
                group_vals[g] = (cb, bg_t, y_off)
            return group_vals[g]

        def prepare(g, t):
            cb, bg_t, y_off = group(g)
            h0 = g * hpg + t * hpt
            ls = slice(h0 * hp, h0 * hp + lanes)
            xt = x_ref[0, :, ls]
            lhs_y, lhs_s, rhs_x = [], [], []
            esc = tot_l = None
            for j in range(hpt):
                h = h0 + j
                col = acum[:, h:h + 1]
                lmat = jnp.exp2(jnp.where(mask, col - row_t[h:h + 1, :], MASK_VALUE))
                lhs_y.append((cb * lmat).astype(BF16))
                lhs_s.append((bg_t * w_t[h:h + 1, :]).astype(BF16))
                own = lane_head == j
                rhs_x.append(jnp.where(own, xt, jnp.zeros_like(xt)))
                esc_j = jnp.exp2(jnp.broadcast_to(col, (q, lanes)))
                tot_j = jnp.broadcast_to(tot[:, h:h + 1], (1, lanes))
                esc = esc_j if j == 0 else jnp.where(own, esc_j, esc)
                tot_l = tot_j if j == 0 else jnp.where(own[0:1], tot_j, tot_l)
            x_bd = jnp.concatenate(rhs_x, axis=0)
            return (ls, xt, jnp.concatenate(lhs_y, axis=1), jnp.concatenate(lhs_s, axis=1), x_bd,
                    esc * y_off[:, t * lanes:(t + 1) * lanes], tot_l)

        def finish(ls, xt, lhs_y, lhs_s, x_bd, y_carry, tot_l):
            y = _dot(lhs_y, x_bd) + y_carry
            if d == 0:
                y = y + dsk_ref[:, ls] * xt.astype(F32)
            y_ref[0, :, ls] = y.astype(BF16)
            st_scr[d, :, ls] = st_scr[d, :, ls] * tot_l + _dot(lhs_s, x_bd)

        tiles = [(g, t) for g in range(groups) for t in range(tpg)]
        ahead = prepare(*tiles[0])
        for i in range(len(tiles)):
            ready = ahead
            if i + 1 < len(tiles):
                ahead = prepare(*tiles[i + 1])
            finish(*ready)

    @pl.when(c == nc - 1)
    def _():
        hout_ref[0] = st_scr[...]


def _ssd(xc, dtt, dt_bias, a_neg, d_e, h0, *, heads, inner):
    b, n, cd = xc.shape
    q = SSM_CHUNK
    assert n % q == 0
    nc = n // q
    groups = SSM_GROUPS
    nstate = (cd - inner) // (2 * groups)
    hp = inner // heads
    assert V7X_LANES % hp == 0 and (heads // groups) % (V7X_LANES // hp) == 0 and nstate <= q
    nh2 = 2 * heads
    bcol = dt_bias.reshape(nh2, 1)
    acol = a_neg.reshape(nh2, 1)
    fwd3 = lambda bi, c: (bi, c, 0)
    bwd3 = lambda bi, c: (bi, nc - 1 - c, 0)
    const2 = lambda bi, c: (0, 0)
    kern = functools.partial(_ssd_kernel, heads=heads, inner=inner, groups=groups, nstate=nstate)
    return pl.pallas_call(
        kern,
        grid=(b, nc),
        in_specs=[pl.BlockSpec((1, q, cd), fwd3),
                  pl.BlockSpec((1, q, cd), bwd3),
                  pl.BlockSpec((nh2, q), lambda bi, c: (0, bi * nc + c)),
                  pl.BlockSpec((nh2, q), lambda bi, c: (0, bi * nc + nc - 1 - c)),
                  pl.BlockSpec((nh2, 1), const2),
                  pl.BlockSpec((nh2, 1), const2),
                  pl.BlockSpec((1, inner), const2),
                  pl.BlockSpec((1, 2, nstate, inner), lambda bi, c: (bi, 0, 0, 0))],
        out_specs=[pl.BlockSpec((1, q, inner), fwd3),
                   pl.BlockSpec((1, q, inner), bwd3),
                   pl.BlockSpec((1, 2, nstate, inner), lambda bi, c: (bi, 0, 0, 0))],
        out_shape=[jax.ShapeDtypeStruct((b, n, inner), BF16),
                   jax.ShapeDtypeStruct((b, n, inner), BF16),
                   jax.ShapeDtypeStruct((b, 2, nstate, inner), F32)],
        scratch_shapes=[pltpu.VMEM((2, nstate, inner), F32)],
        compiler_params=_params("parallel", "arbitrary"),
    )(xc, xc, dtt, dtt, bcol, acol, d_e, h0)


def _mix_kernel(ona_ref, ofn_ref, yf_ref, yb_ref, z_ref, gates_ref, x_ref, g1_ref, nw_ref,
                w1_ref, w2_ref, w3_ref, wo_ref, lg_ref, lb_ref, o_ref, *, groups, alpha):
    d = o_ref.shape[1]
    gate = lambda k: _sigmoid(gates_ref[:, k * d:(k + 1) * d].astype(F32))
    m = gate(0) * _dot(ona_ref[...], w1_ref[...]) + gate(1) * _dot(ofn_ref[...], w2_ref[...])
    gy = (yf_ref[...].astype(F32) + yb_ref[...].astype(F32)) * _silu(z_ref[...].astype(F32))
    gw = gy.shape[1] // groups
    parts = []
    for g in range(groups):
        sl = slice(g * gw, (g + 1) * gw)
        v = gy[:, sl]
        ms = jnp.mean(v * v, axis=-1, keepdims=True)
        parts.append((v * lax.rsqrt(ms + RMS_EPS) * nw_ref[:, sl]).astype(BF16))
    o_ssm = jnp.concatenate(parts, axis=1)
    m = m + gate(2) * _dot(o_ssm, w3_ref[...])
    mix = _dot(m.astype(BF16), wo_ref[...])
    y = alpha * x_ref[...] + g1_ref[0, 0] * mix
    o_ref[...] = _ln(y) * lg_ref[...] + lb_ref[...]


def _mix(x2d, o_na, o_fn, y_f, y_b, p2, mods, norm_w, w1, w2, w3, wo, ln_g, ln_b,
         *, off_z, off_g, inner, alpha):
    m, d = x2d.shape
    na = o_na.shape[1]
    fn = o_fn.shape[1]
    tm = _tile(m, 256, 128)
    assert off_z % inner == 0 and off_g % (3 * d) == 0
    layer = mods.layer
    row = lambda w, blk=0: pl.BlockSpec((tm, w), lambda i: (i, blk))
    full = lambda r, c: pl.BlockSpec((r, c), lambda i: (0, 0))
    stacked = lambda r, c: pl.BlockSpec((None, r, c), lambda i: (layer, 0, 0))
    return pl.pallas_call(
        functools.partial(_mix_kernel, groups=SSM_GROUPS, alpha=alpha),
        grid=(m // tm,),
        in_specs=[row(na), row(fn), row(inner), row(inner),
                  row(inner, off_z // inner), row(3 * d, off_g // (3 * d)), row(d),
                  mods.spec(2, tm),
                  full(1, inner), stacked(na, d), stacked(fn, d), stacked(inner, d), stacked(d, d),
                  full(1, d), full(1, d)],
        out_specs=pl.BlockSpec((tm, d), lambda i: (i, 0)),
        out_shape=jax.ShapeDtypeStruct((m, d), F32),
        compiler_params=_params("parallel"),
    )(o_na, o_fn, y_f, y_b, p2, p2, x2d, mods.table, norm_w, w1, w2, w3, wo,
      ln_g.reshape(1, d), ln_b.reshape(1, d))


def _ffn_kernel(x_ref, sh_ref, sc_ref, g_ref, wg_ref, wu_ref, wd_ref, lg_ref, lb_ref, o_ref,
                h_scr, acc_scr, *, alpha):
    j = pl.program_id(1)
    last = pl.num_programs(1) - 1
    slabs = _slabs(x_ref.shape[0])

    def down(h):
        a = (_silu(_dot(h, wg_ref[...])) * _dot(h, wu_ref[...])).astype(BF16)
        return _dot(a, wd_ref[...])

    @pl.when(j == 0)
    def _():
        def norm(rows):
            h = (_ln(x_ref[rows, :]) * (1.0 + sc_ref[0, 0]) + sh_ref[0, 0]).astype(BF16)
            h_scr[rows, :] = h
            return h

        ahead = norm(slabs[0])
        for r, rows in enumerate(slabs):
            h = ahead
            if r + 1 < len(slabs):
                ahead = norm(slabs[r + 1])
            acc_scr[rows, :] = down(h)

    @pl.when((j > 0) & (j < last))
    def _():
        acc_scr[...] += down(h_scr[...])

    @pl.when(j == last)
    def _():
        ahead = down(h_scr[slabs[0], :])
        for r, rows in enumerate(slabs):
            part = ahead
            if r + 1 < len(slabs):
                ahead = down(h_scr[slabs[r + 1], :])
            y = alpha * x_ref[rows, :] + g_ref[0, 0] * (acc_scr[rows, :] + part)
            o_ref[rows, :] = _ln(y) * lg_ref[...] + lb_ref[...]


def _ffn(x2d, mods, wg, wu, wd, ln_g, ln_b, alpha):
    m, d = x2d.shape
    hid = wg.shape[2]
    tm = _tile(m, 512, 256, 128)
    th = _tile(hid, 512, 256, 128)
    assert hid // th >= 2
    layer = mods.layer
    vec = pl.BlockSpec((1, d), lambda i, j: (0, 0))
    return pl.pallas_call(
        functools.partial(_ffn_kernel, alpha=alpha),
        grid=(m // tm, hid // th),
        in_specs=[pl.BlockSpec((tm, d), lambda i, j: (i, 0)),
                  mods.spec(3, tm), mods.spec(4, tm), mods.spec(5, tm),
                  pl.BlockSpec((None, d, th), lambda i, j: (layer, 0, j)),
                  pl.BlockSpec((None, d, th), lambda i, j: (layer, 0, j)),
                  pl.BlockSpec((None, th, d), lambda i, j: (layer, j, 0)),
                  vec, vec],
        out_specs=pl.BlockSpec((tm, d), lambda i, j: (i, 0)),
        out_shape=jax.ShapeDtypeStruct((m, d), F32),
        scratch_shapes=[pltpu.VMEM((tm, d), BF16), pltpu.VMEM((tm, d), F32)],
        compiler_params=_params("parallel", "arbitrary"),
    )(x2d, mods.table, mods.table, mods.table, wg, wu, wd, ln_g.reshape(1, d), ln_b.reshape(1, d))


def _rope_tables(n_tokens):
    t = np.arange(n_tokens)
    half = NA_HEAD_DIM // 4
    inv = ROPE_THETA ** (-np.arange(half, dtype=np.float32) / half)
    pos = np.stack([t // GRID_W, t % GRID_W], axis=1).astype(np.float32)
    lane = np.arange(NA_HEAD_DIM)
    ang = pos[:, lane // (2 * half)] * inv[lane % half][None, :]
    sign = np.where((lane % (2 * half)) < half, -1.0, 1.0).astype(np.float32)
    return jnp.asarray(np.cos(ang), F32), jnp.asarray(np.sin(ang) * sign, F32)


def _bias_tables(rpb, cases):
    kc = NA_WIN_COLS
    col = np.arange(GRID_W)
    c_start = np.clip(col - kc // 2, 0, GRID_W - kc)
    col_in = (col[None, :] >= c_start[:, None]) & (col[None, :] < c_start[:, None] + kc)
    dc = np.clip(col[None, :] - col[:, None], -(kc - 1), kc - 1) + (kc - 1)
    depth, heads, n_dr, n_dc = rpb.shape
    onehot = (dc[None] == np.arange(n_dc)[:, None, None]).astype(np.float32)
    t = jnp.einsum('lhrc,cqk->lhrqk', rpb.astype(F32) * LOG2E, jnp.asarray(onehot),
                   precision=lax.Precision.HIGHEST)
    t = jnp.where(col_in, t, MASK_VALUE)
    t = jnp.concatenate([t, jnp.full((depth, heads, 1, GRID_W, GRID_W), MASK_VALUE, F32)], axis=2)
    plan = np.full((len(cases), NA_QROWS, NA_KROWS), n_dr, np.int32)
    for ci, (rel_rs, off) in enumerate(cases):
        for qr in range(NA_QROWS):
            for kr in range(NA_KROWS):
                if rel_rs[qr] <= kr < rel_rs[qr] + NA_WIN_ROWS:
                    plan[ci, qr, kr] = kr + off - qr + NA_WIN_ROWS - 1
    g = t[:, :, plan]
    g = jnp.transpose(g, (0, 1, 2, 3, 5, 4, 6))
    return g.reshape(depth, heads, len(cases), NA_QROWS * GRID_W, NA_KROWS * GRID_W)


def kernel(x, c, ctx, c_ctx, w_mod, w_in, na_rpb, ssm_conv_w, ssm_conv_b, ssm_a_log, ssm_dt_bias, ssm_d, ssm_norm_w, w_br_na, w_br_fn, w_br_ssm, w_out, ln1_g, ln1_b, w_ffn_gate, w_ffn_up, w_ffn_down, ln2_g, ln2_b):
    b, s, d = x.shape
    n_ctx = ctx.shape[1]
    depth = w_mod.shape[0]
    na = w_br_na.shape[1]
    fn = w_br_fn.shape[1]
    inner = w_br_ssm.shape[1]
    conv_dim = ssm_conv_w.shape[2]
    ssm_heads = ssm_d.shape[1]
    na_heads = na_rpb.shape[1]
    assert na == na_heads * NA_HEAD_DIM and inner == ssm_heads * SSM_HEAD_DIM and s % GRID_W == 0
    alpha = (2.0 * depth) ** 0.25

    c_u = 3 * na + fn
    c_z = c_u + conv_dim
    c_dt = c_z + inner
    c_g = c_dt + 2 * ssm_heads
    off_q, off_k, off_v, off_u = 0, na, 2 * na, 3 * na
    off_z = c_u
    off_g = off_z + inner
    off_xbc = off_g + 3 * d
    assert off_u % fn == 0

    rows = b + 1
    rows_pad = -(-rows // MOD_ROWS_ALIGN) * MOD_ROWS_ALIGN
    c_rows = jnp.concatenate([c, c_ctx[None, :], jnp.zeros((rows_pad - rows, d), F32)], axis=0)
    mod_table = _mods(c_rows, w_mod).reshape(depth, rows_pad, 1, 6 * d)

    cos_t, sin_t = _rope_tables(s)
    na_blocks, na_cases = _na_plan(s // GRID_W)
    dft_x = _dft_tables(s, fn // FN_GROUPS)
    dft_c = _dft_tables(n_ctx, fn // FN_GROUPS)
    nstate = (conv_dim - inner) // (2 * SSM_GROUPS)
    zero_state = jnp.zeros((b, 2, nstate, inner), F32)

    x2 = x.reshape(b * s, d)
    c2 = ctx.reshape(b * n_ctx, d)
    flat = lambda t: t.reshape(-1, t.shape[-1])
    n_main = off_xbc + conv_dim
    w_main = _relayout_w_in(jnp.swapaxes(w_in, 1, 2),
                            [(0, c_u), (c_z, inner), (c_g, 3 * d), (c_u, conv_dim), (c_dt, RELAYOUT_TILE)])
    w_dtt_all = w_main[:, n_main:n_main + 2 * ssm_heads, :]
    w1, w2, w3 = w_br_na.astype(BF16), w_br_fn.astype(BF16), w_br_ssm.astype(BF16)
    wo = w_out.astype(BF16)
    wg, wu, wd = w_ffn_gate.astype(BF16), w_ffn_up.astype(BF16), w_ffn_down.astype(BF16)
    bias_all = _bias_tables(na_rpb, na_cases)
    mix_kw = dict(off_z=off_z, off_g=off_g, inner=inner, alpha=alpha)
    for l in range(depth):
        with_ctx = l < depth - 1
        w_dtt = w_dtt_all[l]
        mx = _Mods(mod_table, l, 0, s)
        mc = _Mods(mod_table, l, b, b * n_ctx)
        a_neg = -jnp.exp(ssm_a_log[l].astype(F32))
        d_e = jnp.repeat(ssm_d[l].astype(F32), SSM_HEAD_DIM).reshape(1, inner)
        norm_w = ssm_norm_w[l].astype(F32).reshape(1, inner)

        px, dttx = _inproj(x2, mx, w_main, n_main, w_dtt)
        pc, dttc = _inproj(c2, mc, w_main, n_main, w_dtt)
        px3 = px.reshape(b, s, -1)
        pc3 = pc.reshape(b, n_ctx, -1)

        o_na = _na_latent(px3, pc3, cos_t, sin_t, bias_all, l, na_blocks, heads=na_heads,
                          off_q=off_q, off_k=off_k, off_v=off_v)
        o_fn = _fourier(px3, *dft_x, off_u=off_u, width=fn)
        xc_c = _conv(pc3, ssm_conv_w[l], ssm_conv_b[l], off_xbc=off_xbc)
        yf_c, yb_c, h_ctx = _ssd(xc_c, dttc, ssm_dt_bias[l], a_neg, d_e, zero_state, heads=ssm_heads, inner=inner)
        xc_x = _conv(px3, ssm_conv_w[l], ssm_conv_b[l], off_xbc=off_xbc)
        yf_x, yb_x, _ = _ssd(xc_x, dttx, ssm_dt_bias[l], a_neg, d_e, h_ctx, heads=ssm_heads, inner=inner)

        x2 = _mix(x2, flat(o_na), flat(o_fn), flat(yf_x), flat(yb_x), px, mx, norm_w,
                  w1, w2, w3, wo, ln1_g[l], ln1_b[l], **mix_kw)
        x2 = _ffn(x2, mx, wg, wu, wd, ln2_g[l], ln2_b[l], alpha)
        if with_ctx:
            o_na_c = _ctx_attn(pc3, heads=na_heads, off_q=off_q, off_k=off_k, off_v=off_v)
            o_fn_c = _fourier(pc3, *dft_c, off_u=off_u, width=fn)
            c2 = _mix(c2, flat(o_na_c), flat(o_fn_c), flat(yf_c), flat(yb_c), pc, mc, norm_w,
                      w1, w2, w3, wo, ln1_g[l], ln1_b[l], **mix_kw)
            c2 = _ffn(c2, mc, wg, wu, wd, ln2_g[l], ln2_b[l], alpha)
    return x2.reshape(b, s, d)
```
